```python
import functools
import jax, jax.numpy as jnp
from jax import lax
import numpy as np

D_MODEL = 1024
BATCH = 4
SEQ = 8192
DEPTH = 1
DEC_BATCH = 128
DEC_SEQ = 8
PAST_LEN = 8192
PAGE_SIZE = 128

CONV_CH = D_MODEL // 2
CONV_W = 31
N_HEADS = 8
HEAD_DIM = 64
N_KV_HEADS = 4
GROUP = N_HEADS // N_KV_HEADS
ATTN_W = N_HEADS * HEAD_DIM
KV_W = N_KV_HEADS * HEAD_DIM
IDX_HEADS = 8
IDX_DIM = 64
IDX_SCALE = (IDX_HEADS * IDX_DIM) ** -0.5
TOPK_MAX = 256
Q_BLOCK = 128
N_GROUPS = 4
EXP_PER_GROUP = 8
N_EXPERTS = N_GROUPS * EXP_PER_GROUP
EXPERT_FF = 256
TOPK_EXP = 2
MOE_BLOCK = 256
EPS = 1e-6
PROJ_SIZES = (2 * CONV_CH, ATTN_W, KV_W, KV_W, IDX_HEADS * IDX_DIM, IDX_DIM, IDX_HEADS, 2 * D_MODEL)
IN_W = sum(PROJ_SIZES)

kernel_name = 'hybrid_conv_dsa_hmoe_adaln_step'

F32 = jnp.float32


def rms_norm(x, g):
    xf = x.astype(F32)
    y = xf * lax.rsqrt(jnp.mean(xf * xf, axis=-1, keepdims=True) + EPS)
    return (y * g).astype(x.dtype)


def layer_norm(x, g, b):
    xf = x.astype(F32)
    mu = jnp.mean(xf, axis=-1, keepdims=True)
    var = jnp.mean(jnp.square(xf - mu), axis=-1, keepdims=True)
    return ((xf - mu) * lax.rsqrt(var + EPS) * g + b).astype(x.dtype)


def adaln(c, w_ada, b_ada):
    mod = jax.nn.silu(c) @ w_ada + b_ada
    return [m[:, None, :] for m in jnp.split(mod, 6, axis=-1)]


def split_proj(u):
    offs, acc = [], 0
    for s in PROJ_SIZES[:-1]:
        acc += s
        offs.append(acc)
    return jnp.split(u, offs, axis=-1)


def conv_branch(cpad, conv_w, conv_b, ln_g, ln_b, w_conv_out):
    y = lax.conv_general_dilated(cpad, conv_w[:, None, :], window_strides=(1,), padding='VALID',
                                 dimension_numbers=('NWC', 'WIO', 'NWC'),
                                 feature_group_count=CONV_CH) + conv_b
    y = jax.nn.silu(layer_norm(y, ln_g, ln_b))
    return y @ w_conv_out


def index_scores(qi, wi, ki):
    s = jnp.einsum('bthd,bsd->bths', qi.astype(F32), ki.astype(F32))
    return jnp.einsum('bth,bths->bts', wi.astype(F32), jax.nn.relu(s)) * IDX_SCALE


def select_keys(scores, q_pos, n_keep):
    L = scores.shape[-1]
    causal = jnp.arange(L)[None, None, :] <= q_pos[None, :, None]
    _, idx = lax.top_k(jnp.where(causal, scores, -jnp.inf), n_keep)
    valid = idx <= q_pos[None, :, None]
    return idx, valid


def attend(q, k_sel, v_sel, valid):
    B, Tq = q.shape[:2]
    qg = q.reshape(B, Tq, N_KV_HEADS, GROUP, HEAD_DIM).astype(F32)
    s = jnp.einsum('btkgd,btskd->btkgs', qg, k_sel.astype(F32)) * (HEAD_DIM ** -0.5)
    s = jnp.where(valid[:, :, None, None, :], s, -jnp.inf)
    p = jax.nn.softmax(s, axis=-1)
    o = jnp.einsum('btkgs,btskd->btkgd', p, v_sel.astype(F32))
    return o.reshape(B, Tq, ATTN_W).astype(q.dtype)


def _take_rows(a, i):
    return jax.vmap(lambda ab, ib: ab[ib])(a, i)


def prompt_attention(q, k, v, qi, wi, ki):
    B, S = q.shape[:2]
    n_keep = min(TOPK_MAX, S // 4)
    nb = S // Q_BLOCK

    def to_blocks(a):
        return jnp.moveaxis(a.reshape(B, nb, Q_BLOCK, *a.shape[2:]), 1, 0)

    def block(args):
        qb, qib, wib, pos = args
        idx, valid = select_keys(index_scores(qib, wib, ki), pos, n_keep)
        return attend(qb, _take_rows(k, idx), _take_rows(v, idx), valid)

    pos = jnp.arange(S).reshape(nb, Q_BLOCK)
    out = lax.map(block, (to_blocks(q), to_blocks(qi), to_blocks(wi), pos))
    return jnp.moveaxis(out, 0, 1).reshape(B, S, ATTN_W)


def sample_attention(q, k_new, v_new, qi, wi, ki_new, layer, cache_k, cache_v, cache_kidx, page_table):
    B, T = q.shape[:2]
    n_pages = page_table.shape[1]
    past = n_pages * PAGE_SIZE
    n_keep = min(TOPK_MAX, (past + T) // 4)
    ki_past = cache_kidx[layer, page_table].reshape(B, past, IDX_DIM)
    ki_all = jnp.concatenate([ki_past, ki_new.astype(ki_past.dtype)], axis=1)
    q_pos = past + jnp.arange(T)
    idx, valid = select_keys(index_scores(qi, wi, ki_all), q_pos, n_keep)
    in_past = idx < past
    pidx = jnp.minimum(idx, past - 1)
    phys = jnp.take_along_axis(page_table, (pidx // PAGE_SIZE).reshape(B, -1), axis=1).reshape(idx.shape)
    off = pidx % PAGE_SIZE
    nidx = jnp.clip(idx - past, 0, T - 1)

    def gather(cache, new):
        old_rows = cache[layer, phys, off]
        return jnp.where(in_past[..., None, None], old_rows, _take_rows(new, nidx).astype(old_rows.dtype))

    return attend(q, gather(cache_k, k_new), gather(cache_v, v_new), valid)


def moe(h, w_grp, b_grp, w_exp, b_exp, w1, w3, w2):
    N = h.shape[0]
    p_grp = jax.nn.softmax((h @ w_grp + b_grp).astype(F32), axis=-1)
    g_top, g_idx = lax.top_k(p_grp, 1)
    exp_logits = (h @ w_exp + b_exp).astype(F32).reshape(N, N_GROUPS, EXP_PER_GROUP)
    in_grp = jnp.take_along_axis(exp_logits, g_idx[:, :, None], axis=1)[:, 0]
    e_top, e_idx = lax.top_k(jax.nn.softmax(in_grp, axis=-1), TOPK_EXP)
    wts = g_top * e_top / jnp.sum(e_top, axis=-1, keepdims=True)
    ids = g_idx * EXP_PER_GROUP + e_idx
    gate = jnp.sum(jax.nn.one_hot(ids, N_EXPERTS, dtype=F32) * wts[..., None], axis=1)
    n_pad = (-N) % MOE_BLOCK
    hb = jnp.pad(h, ((0, n_pad), (0, 0))).reshape(-1, MOE_BLOCK, D_MODEL)
    gb = jnp.pad(gate, ((0, n_pad), (0, 0))).reshape(-1, MOE_BLOCK, N_EXPERTS).astype(h.dtype)

    def block(args):
        xb, gblk = args
        a = jnp.einsum('nd,edf->nef', xb, w1)
        b = jnp.einsum('nd,edf->nef', xb, w3)
        hid = jax.nn.silu(a) * b * gblk[..., None]
        return jnp.einsum('nef,efd->nd', hid, w2)

    return lax.map(block, (hb, gb)).reshape(-1, D_MODEL)[:N]


def layer_forward(x, c, conv_past, attn_fn, p):
    B, T, _ = x.shape
    sh1, sc1, g1, sh2, sc2, g2 = adaln(c, p['w_ada'], p['b_ada'])
    h = rms_norm(x, p['norm1_g']) * (1 + sc1) + sh1
    glu, q, k, v, qi, ki, wi, gates = split_proj(h @ p['w_in'])
    ga, gb = jnp.split(glu, 2, axis=-1)
    cin = ga * jax.nn.sigmoid(gb)
    cpad = jnp.concatenate([conv_past.astype(cin.dtype), cin], axis=1)
    y_conv = conv_branch(cpad, p['conv_w'], p['conv_b'], p['ln_g'], p['ln_b'], p['w_conv_out'])
    q = rms_norm(q.reshape(B, T, N_HEADS, HEAD_DIM), p['q_norm_g'])
    k = rms_norm(k.reshape(B, T, N_KV_HEADS, HEAD_DIM), p['k_norm_g'])
    v = v.reshape(B, T, N_KV_HEADS, HEAD_DIM)
    qi = qi.reshape(B, T, IDX_HEADS, IDX_DIM)
    y_attn = attn_fn(q, k, v, qi, wi, ki) @ p['w_attn_o']
    g_conv, g_attn = jnp.split(gates, 2, axis=-1)
    m = jax.nn.sigmoid(g_conv) * y_conv + jax.nn.sigmoid(g_attn) * y_attn
    x = x + g1 * (m @ p['w_out'])
    h2 = rms_norm(x, p['norm2_g']) * (1 + sc2) + sh2
    y = moe(h2.reshape(B * T, D_MODEL), p['w_grp'], p['b_grp'], p['w_exp'], p['b_exp'],
            p['w1'], p['w3'], p['w2']).reshape(B, T, D_MODEL)
    x = x + g2 * y
    return x, k, v, ki, cpad[:, -(CONV_W - 1):]


def setup_inputs(seed: int = 0) -> dict:
    key = jax.random.key(seed)
    ks = jax.random.split(key, 40)
    n_pages = PAST_LEN // PAGE_SIZE
    n_used = DEC_BATCH * n_pages
    n_pool = n_used + (n_used + 3) // 4

    def nrm(k, shape, scale):
        return jax.random.normal(k, shape, F32) * scale

    perm = jax.random.permutation(ks[9], n_pool).astype(jnp.int32)
    page_table = perm[:n_used].reshape(DEC_BATCH, n_pages)
    L = DEPTH
    return {
        'x_prompt': nrm(ks[0], (BATCH, SEQ, D_MODEL), 1.0),
        'x_sample': nrm(ks[1], (DEC_BATCH, DEC_SEQ, D_MODEL), 1.0),
        'cache_k': nrm(ks[2], (L, n_pool, PAGE_SIZE, N_KV_HEADS, HEAD_DIM), 1.0),
        'cache_v': nrm(ks[3], (L, n_pool, PAGE_SIZE, N_KV_HEADS, HEAD_DIM), 1.0),
        'cache_kidx': nrm(ks[4], (L, n_pool, PAGE_SIZE, IDX_DIM), 1.0),
        'state_conv': nrm(ks[5], (L, DEC_BATCH, CONV_W - 1, CONV_CH), 0.5),
        'page_table': page_table,
        'c_prompt': nrm(ks[6], (BATCH, D_MODEL), 1.0),
        'c_sample': nrm(ks[7], (DEC_BATCH, D_MODEL), 1.0),
        'w_ada': nrm(ks[10], (L, D_MODEL, 6 * D_MODEL), 0.3 * D_MODEL ** -0.5),
        'b_ada': nrm(ks[11], (L, 6 * D_MODEL), 0.01),
        'norm1_g': 1.0 + nrm(ks[12], (L, D_MODEL), 0.02),
        'w_in': nrm(ks[13], (L, D_MODEL, IN_W), D_MODEL ** -0.5),
        'q_norm_g': 1.0 + nrm(ks[14], (L, HEAD_DIM), 0.02),
        'k_norm_g': 1.0 + nrm(ks[15], (L, HEAD_DIM), 0.02),
        'conv_w': nrm(ks[16], (L, CONV_W, CONV_CH), CONV_W ** -0.5),
        'conv_b': nrm(ks[17], (L, CONV_CH), 0.01),
        'ln_g': 1.0 + nrm(ks[18], (L, CONV_CH), 0.02),
        'ln_b': nrm(ks[19], (L, CONV_CH), 0.01),
        'w_conv_out': nrm(ks[20], (L, CONV_CH, D_MODEL), CONV_CH ** -0.5),
        'w_attn_o': nrm(ks[21], (L, ATTN_W, D_MODEL), ATTN_W ** -0.5),
        'w_out': nrm(ks[22], (L, D_MODEL, D_MODEL), D_MODEL ** -0.5),
        'norm2_g': 1.0 + nrm(ks[23], (L, D_MODEL), 0.02),
        'w_grp': nrm(ks[24], (L, D_MODEL, N_GROUPS), D_MODEL ** -0.5),
        'b_grp': nrm(ks[25], (L, N_GROUPS), 0.01),
        'w_exp': nrm(ks[26], (L, D_MODEL, N_EXPERTS), D_MODEL ** -0.5),
        'b_exp': nrm(ks[27], (L, N_EXPERTS), 0.01),
        'w1': nrm(ks[28], (L, N_EXPERTS, D_MODEL, EXPERT_FF), D_MODEL ** -0.5),
        'w3': nrm(ks[29], (L, N_EXPERTS, D_MODEL, EXPERT_FF), D_MODEL ** -0.5),
        'w2': nrm(ks[30], (L, N_EXPERTS, EXPERT_FF, D_MODEL), EXPERT_FF ** -0.5),
    }


def reference(x_prompt, x_sample, cache_k, cache_v, cache_kidx, state_conv, page_table,
              c_prompt, c_sample, w_ada, b_ada, norm1_g, w_in, q_norm_g, k_norm_g,
              conv_w, conv_b, ln_g, ln_b, w_conv_out, w_attn_o, w_out, norm2_g,
              w_grp, b_grp, w_exp, b_exp, w1, w3, w2):
    xp, xs = x_prompt, x_sample
    k_p, v_p, ki_p, cv_p = [], [], [], []
    k_s, v_s, ki_s, cv_s = [], [], [], []
    for l in range(DEPTH):
        p = dict(w_ada=w_ada[l], b_ada=b_ada[l], norm1_g=norm1_g[l], w_in=w_in[l],
                 q_norm_g=q_norm_g[l], k_norm_g=k_norm_g[l], conv_w=conv_w[l], conv_b=conv_b[l],
                 ln_g=ln_g[l], ln_b=ln_b[l], w_conv_out=w_conv_out[l], w_attn_o=w_attn_o[l],
                 w_out=w_out[l], norm2_g=norm2_g[l], w_grp=w_grp[l], b_grp=b_grp[l],
                 w_exp=w_exp[l], b_exp=b_exp[l], w1=w1[l], w3=w3[l], w2=w2[l])
        conv_zero = jnp.zeros((xp.shape[0], CONV_W - 1, CONV_CH), xp.dtype)
        xp, kl, vl, kil, cvl = layer_forward(xp, c_prompt, conv_zero, prompt_attention, p)
        k_p.append(kl); v_p.append(vl); ki_p.append(kil); cv_p.append(cvl)
        attn_s = functools.partial(sample_attention, layer=l, cache_k=cache_k, cache_v=cache_v,
                                   cache_kidx=cache_kidx, page_table=page_table)
        xs, kl, vl, kil, cvl = layer_forward(xs, c_sample, state_conv[l], attn_s, p)
        k_s.append(kl); v_s.append(vl); ki_s.append(kil); cv_s.append(cvl)
    return (xp, xs, jnp.stack(k_p), jnp.stack(v_p), jnp.stack(ki_p), jnp.stack(cv_p),
            jnp.stack(k_s), jnp.stack(v_s), jnp.stack(ki_s), jnp.stack(cv_s))
```

```python
import functools

import jax
import jax.numpy as jnp
from jax import lax
from jax.experimental import pallas as pl
from jax.experimental.pallas import tpu as pltpu

F32 = jnp.float32
BF16 = jnp.bfloat16

D_MODEL = 1024
CONV_CH = 512
CONV_W = 31
N_HEADS = 8
HEAD_DIM = 64
N_KV_HEADS = 4
KV_W = N_KV_HEADS * HEAD_DIM
IDX_HEADS = 8
IDX_DIM = 64
IDX_SCALE = (IDX_HEADS * IDX_DIM) ** -0.5
TOPK_MAX = 256
PAGE_SIZE = 128
N_GROUPS = 4
EXP_PER_GROUP = 8
N_EXPERTS = 32
EXPERT_FF = 256
EPS = 1e-6

LANES = 128
NEG_BIG = -1e30
VMEM_LIMIT = 56 * 1024 * 1024

C_GLU = 0
C_Q = C_GLU + 2 * CONV_CH
C_K = C_Q + N_HEADS * LANES
C_V = C_K + KV_W
C_QI = C_V + KV_W
C_KIW = C_QI + IDX_HEADS * LANES
C_GATES = C_KIW + LANES
IN_PACK = C_GATES + 2 * D_MODEL

N_BISECT = 22


def _cparams(sem):
    return pltpu.CompilerParams(dimension_semantics=sem, vmem_limit_bytes=VMEM_LIMIT)


def _sigmoid(x):
    return 1.0 / (1.0 + jnp.exp(-x))


def _silu(x):
    return x * _sigmoid(x)


def _const_spec(shape):
    nd = len(shape)
    return pl.BlockSpec(shape, lambda *_: (0,) * nd)


def _ada_kernel(c_ref, w_ref, b_ref, o_ref):
    s = _silu(c_ref[...])
    o_ref[...] = jnp.dot(s.astype(BF16), w_ref[...], preferred_element_type=F32) + b_ref[...]


def _ada(c, w_ada, b_ada):
    r = c.shape[0]
    tn = 1536
    return pl.pallas_call(
        _ada_kernel,
        grid=(6 * D_MODEL // tn,),
        in_specs=[pl.BlockSpec((r, D_MODEL), lambda j: (0, 0)),
                  pl.BlockSpec((D_MODEL, tn), lambda j: (0, j)),
                  pl.BlockSpec((1, tn), lambda j: (0, j))],
        out_specs=pl.BlockSpec((r, tn), lambda j: (0, j)),
        out_shape=jax.ShapeDtypeStruct((r, 6 * D_MODEL), F32),
        compiler_params=_cparams(("arbitrary",)),
        name="adaln_mod",
    )(c, w_ada.astype(BF16), b_ada.reshape(1, -1))


def _inproj_kernel(x_ref, sc_ref, sh_ref, g_ref, w_ref, qg_ref, kg_ref, ones_ref,
                   cin_ref, q_ref, k_ref, v_ref, kb_ref, vb_ref, qi_ref, kiw_ref, kib_ref, gates_ref):
    bb, tt, _ = x_ref.shape
    x = x_ref[...]
    ms = jnp.mean(x * x, axis=-1, keepdims=True)
    h = x * lax.rsqrt(ms + EPS) * g_ref[...] * (1.0 + sc_ref[...]) + sh_ref[...]
    hb = h.reshape(bb * tt, D_MODEL).astype(BF16)

    def proj(c0, width):
        return jnp.dot(hb, w_ref[:, c0:c0 + width], preferred_element_type=F32)

    glu = proj(C_GLU, 2 * CONV_CH)
    cin_ref[...] = glu[:, :CONV_CH] * _sigmoid(glu[:, CONV_CH:])

    for hd in range(N_HEADS):
        qh = proj(C_Q + hd * LANES, LANES)
        msq = jnp.sum(qh * qh, axis=-1, keepdims=True) * (1.0 / HEAD_DIM)
        qn = qh * lax.rsqrt(msq + EPS) * qg_ref[:, hd * LANES:(hd + 1) * LANES]
        q_ref[:, hd * LANES:(hd + 1) * LANES] = (qn * (HEAD_DIM ** -0.5)).astype(BF16)

    k = proj(C_K, KV_W)
    k2 = k * k
    k2_hi = k2.astype(BF16)
    k2_lo = (k2 - k2_hi.astype(F32)).astype(BF16)
    ksum = (jnp.dot(k2_hi, ones_ref[...], preferred_element_type=F32)
            + jnp.dot(k2_lo, ones_ref[...], preferred_element_type=F32))
    kn = k * lax.rsqrt(ksum * (1.0 / HEAD_DIM) + EPS) * kg_ref[...]
    k_ref[...] = kn
    kb_ref[...] = kn.astype(BF16)

    v = proj(C_V, KV_W)
    v_ref[...] = v
    vb_ref[...] = v.astype(BF16)

    qi_ref[...] = proj(C_QI, IDX_HEADS * LANES).astype(BF16)

    kiw = proj(C_KIW, LANES)
    kiw_ref[...] = kiw
    lane = lax.broadcasted_iota(jnp.int32, kiw.shape, 1)
    kib_ref[...] = jnp.where(lane < IDX_DIM, kiw, 0.0).astype(BF16)

    gates_ref[...] = _sigmoid(proj(C_GATES, 2 * D_MODEL))


def _pack_w_in(w_in):
    o = 0
    segs = {}
    for name, width in (("glu", 2 * CONV_CH), ("q", N_HEADS * HEAD_DIM), ("k", KV_W), ("v", KV_W),
                        ("qi", IDX_HEADS * IDX_DIM), ("ki", IDX_DIM), ("wi", IDX_HEADS), ("gates", 2 * D_MODEL)):
        segs[name] = w_in[:, o:o + width]
        o += width
    zeros64 = jnp.zeros((D_MODEL, HEAD_DIM), F32)
    cols = [segs["glu"]]
    for hd in range(N_HEADS):
        wq = segs["q"][:, hd * HEAD_DIM:(hd + 1) * HEAD_DIM]
        cols += [zeros64, wq] if (hd // 2) % 2 else [wq, zeros64]
    cols += [segs["k"], segs["v"]]
    for hd in range(IDX_HEADS):
        cols += [segs["qi"][:, hd * IDX_DIM:(hd + 1) * IDX_DIM], zeros64]
    cols += [segs["ki"], segs["wi"], jnp.zeros((D_MODEL, LANES - IDX_DIM - IDX_HEADS), F32)]
    cols += [segs["gates"]]
    return jnp.concatenate(cols, axis=1).astype(BF16)


def _pack_q_gain(q_norm_g):
    z = jnp.zeros((HEAD_DIM,), F32)
    parts = []
    for hd in range(N_HEADS):
        parts += [z, q_norm_g] if (hd // 2) % 2 else [q_norm_g, z]
    return jnp.concatenate(parts).reshape(1, N_HEADS * LANES)


def _inproj(x, sc1, sh1, norm1_g, w_pack, q_norm_g, k_norm_g, bb, tt):
    B, T, _ = x.shape
    n = B * T
    tm = bb * tt
    nb_t = T // tt
    grid = (B // bb, nb_t)
    row = lambda i, j: (i * nb_t + j, 0)
    ones = (lax.broadcasted_iota(jnp.int32, (KV_W, KV_W), 0) // HEAD_DIM
            == lax.broadcasted_iota(jnp.int32, (KV_W, KV_W), 1) // HEAD_DIM).astype(BF16)
    outs = [("cin", CONV_CH, F32), ("q", N_HEADS * LANES, BF16), ("k", KV_W, F32), ("v", KV_W, F32),
            ("kb", KV_W, BF16), ("vb", KV_W, BF16), ("qi", IDX_HEADS * LANES, BF16), ("kiw", LANES, F32),
            ("kib", LANES, BF16), ("gates", 2 * D_MODEL, F32)]
    res = pl.pallas_call(
        _inproj_kernel,
        grid=grid,
        in_specs=[pl.BlockSpec((bb, tt, D_MODEL), lambda i, j: (i, j, 0)),
                  pl.BlockSpec((bb, 1, D_MODEL), lambda i, j: (i, 0, 0)),
                  pl.BlockSpec((bb, 1, D_MODEL), lambda i, j: (i, 0, 0)),
                  _const_spec((1, 1, D_MODEL)),
                  _const_spec((D_MODEL, IN_PACK)),
                  _const_spec((1, N_HEADS * LANES)),
                  _const_spec((1, KV_W)),
                  _const_spec((KV_W, KV_W))],
        out_specs=[pl.BlockSpec((tm, w), row) for _, w, _ in outs],
        out_shape=[jax.ShapeDtypeStruct((n, w), dt) for _, w, dt in outs],
        compiler_params=_cparams(("arbitrary", "arbitrary")),
        name="in_proj",
    )(x, sc1, sh1, norm1_g.reshape(1, 1, D_MODEL), w_pack, _pack_q_gain(q_norm_g),
      jnp.tile(k_norm_g, N_KV_HEADS).reshape(1, KV_W), ones)
    return {name: r for (name, _, _), r in zip(outs, res)}


CONV_HALO = 32
CONV_ROWS = 32


def _ln_silu(y, g, b):
    mu = jnp.mean(y, axis=-1, keepdims=True)
    yc = y - mu
    var = jnp.mean(yc * yc, axis=-1, keepdims=True)
    return _silu(yc * lax.rsqrt(var + EPS) * g + b)


def _conv_prompt_kernel(cin_ref, w_ref, b_ref, g_ref, lb_ref, o_ref, win_ref):
    tc = cin_ref.shape[0]

    @pl.when(pl.program_id(1) == 0)
    def _():
        win_ref[0:CONV_HALO, :] = jnp.zeros((CONV_HALO, CONV_CH), F32)

    win_ref[CONV_HALO:, :] = cin_ref[...]
    off = CONV_HALO - (CONV_W - 1)
    for r0 in range(0, tc, CONV_ROWS):
        acc = jnp.zeros((CONV_ROWS, CONV_CH), F32) + b_ref[...]
        for j in range(CONV_W):
            acc = acc + win_ref[r0 + off + j:r0 + off + j + CONV_ROWS, :] * w_ref[j:j + 1, :]
        o_ref[r0:r0 + CONV_ROWS, :] = _ln_silu(acc, g_ref[...], lb_ref[...]).astype(BF16)
    win_ref[0:CONV_HALO, :] = cin_ref[tc - CONV_HALO:, :]


def _conv_prompt(cin, B, S, conv_w, conv_b, ln_g, ln_b):
    tc = 256
    nt = S // tc
    return pl.pallas_call(
        _conv_prompt_kernel,
        grid=(B, nt),
        in_specs=[pl.BlockSpec((tc, CONV_CH), lambda b, j: (b * nt + j, 0)),
                  _const_spec((CONV_W, CONV_CH)), _const_spec((1, CONV_CH)),
                  _const_spec((1, CONV_CH)), _const_spec((1, CONV_CH))],
        out_specs=pl.BlockSpec((tc, CONV_CH), lambda b, j: (b * nt + j, 0)),
        out_shape=jax.ShapeDtypeStruct((B * S, CONV_CH), BF16),
        scratch_shapes=[pltpu.VMEM((CONV_HALO + tc, CONV_CH), F32)],
        compiler_params=_cparams(("arbitrary", "arbitrary")),
        name="conv_prompt",
    )(cin, conv_w, conv_b.reshape(1, -1), ln_g.reshape(1, -1), ln_b.reshape(1, -1))


def _conv_sample_kernel(cpad_ref, w_ref, b_ref, g_ref, lb_ref, o_ref):
    bb, tp, _ = cpad_ref.shape
    t = tp - (CONV_W - 1)
    for b in range(bb):
        acc = jnp.zeros((t, CONV_CH), F32) + b_ref[...]
        for j in range(CONV_W):
            acc = acc + cpad_ref[b, j:j + t, :] * w_ref[j:j + 1, :]
        o_ref[b] = _ln_silu(acc, g_ref[...], lb_ref[...]).astype(BF16)


def _conv_sample(cpad, conv_w, conv_b, ln_g, ln_b):
    B, tp, _ = cpad.shape
    t = tp - (CONV_W - 1)
    bb = 8
    return pl.pallas_call(
        _conv_sample_kernel,
        grid=(B // bb,),
        in_specs=[pl.BlockSpec((bb, tp, CONV_CH), lambda i: (i, 0, 0)),
                  _const_spec((CONV_W, CONV_CH)), _const_spec((1, CONV_CH)),
                  _const_spec((1, CONV_CH)), _const_spec((1, CONV_CH))],
        out_specs=pl.BlockSpec((bb, t, CONV_CH), lambda i: (i, 0, 0)),
        out_shape=jax.ShapeDtypeStruct((B, t, CONV_CH), BF16),
        compiler_params=_cparams(("arbitrary",)),
        name="conv_sample",
    )(cpad, conv_w, conv_b.reshape(1, -1), ln_g.reshape(1, -1), ln_b.reshape(1, -1))


R_EXP = 0
R_GRP = N_EXPERTS


def _lane_min_index(mask, lane):
    return jnp.min(jnp.where(mask, lane, float(LANES)), axis=-1, keepdims=True)


def _router_gate(logits):
    lane = lax.broadcasted_iota(jnp.int32, logits.shape, 1).astype(F32)
    is_grp = (lane >= R_GRP) & (lane < R_GRP + N_GROUPS)
    gl = jnp.where(is_grp, logits, -jnp.inf)
    gmax = jnp.max(gl, axis=-1, keepdims=True)
    gexp = jnp.where(is_grp, jnp.exp(gl - gmax), 0.0)
    p_grp = gexp / jnp.sum(gexp, axis=-1, keepdims=True)
    g_top = jnp.max(p_grp, axis=-1, keepdims=True)
    g_idx = _lane_min_index(is_grp & (p_grp == g_top), lane) - R_GRP
    in_grp = (lane >= g_idx * EXP_PER_GROUP) & (lane < (g_idx + 1) * EXP_PER_GROUP)
    el = jnp.where(in_grp, logits, -jnp.inf)
    emax = jnp.max(el, axis=-1, keepdims=True)
    eexp = jnp.where(in_grp, jnp.exp(el - emax), 0.0)
    p_exp = eexp / jnp.sum(eexp, axis=-1, keepdims=True)
    p1 = jnp.max(jnp.where(in_grp, p_exp, -1.0), axis=-1, keepdims=True)
    i1 = _lane_min_index(in_grp & (p_exp == p1), lane)
    rest = in_grp & (lane != i1)
    p2 = jnp.max(jnp.where(rest, p_exp, -1.0), axis=-1, keepdims=True)
    i2 = _lane_min_index(rest & (p_exp == p2), lane)
    tot = p1 + p2
    return jnp.where(lane == i1, g_top * p1 / tot, 0.0) + jnp.where(lane == i2, g_top * p2 / tot, 0.0)


def _merge_kernel(x_ref, g1_ref, sc2_ref, sh2_ref, yc_ref, at_ref, gates_ref, wco_ref, wao_ref, wout_ref,
                  n2g_ref, wr_ref, br_ref, x1_ref, h2_ref, gate_ref):
    bb, tt, _ = x_ref.shape
    y_conv = jnp.dot(yc_ref[...], wco_ref[...], preferred_element_type=F32)
    y_attn = jnp.dot(at_ref[...], wao_ref[...], preferred_element_type=F32)
    m = gates_ref[:, :D_MODEL] * y_conv + gates_ref[:, D_MODEL:] * y_attn
    mo = jnp.dot(m.astype(BF16), wout_ref[...], preferred_element_type=F32)
    x1 = x_ref[...] + g1_ref[...] * mo.reshape(bb, tt, D_MODEL)
    x1_ref[...] = x1
    ms = jnp.mean(x1 * x1, axis=-1, keepdims=True)
    h2 = (x1 * lax.rsqrt(ms + EPS) * n2g_ref[...] * (1.0 + sc2_ref[...]) + sh2_ref[...]).reshape(bb * tt, D_MODEL)
    h2_ref[...] = h2.astype(BF16)
    logits = jnp.dot(h2, wr_ref[...], preferred_element_type=F32, precision=lax.Precision.HIGHEST) + br_ref[...]
    gate_ref[...] = _router_gate(logits)


def _merge(x, g1, sc2, sh2, yc, attn, gates, w_conv_out, w_attn_o, w_out, norm2_g, w_grp, b_grp, w_exp, b_exp,
           bb, tt):
    B, T, _ = x.shape
    n = B * T
    tm = bb * tt
    nb_t = T // tt
    row = lambda i, j: (i * nb_t + j, 0)
    mod = lambda i, j: (i, 0, 0)
    pad = jnp.zeros((D_MODEL, LANES - N_EXPERTS - N_GROUPS), F32)
    wr = jnp.concatenate([w_exp, w_grp, pad], axis=1)
    br = jnp.concatenate([b_exp, b_grp, jnp.zeros((LANES - N_EXPERTS - N_GROUPS,), F32)]).reshape(1, LANES)
    return pl.pallas_call(
        _merge_kernel,
        grid=(B // bb, nb_t),
        in_specs=[pl.BlockSpec((bb, tt, D_MODEL), lambda i, j: (i, j, 0)),
                  pl.BlockSpec((bb, 1, D_MODEL), mod), pl.BlockSpec((bb, 1, D_MODEL), mod),
                  pl.BlockSpec((bb, 1, D_MODEL), mod),
                  pl.BlockSpec((tm, CONV_CH), row), pl.BlockSpec((tm, N_HEADS * HEAD_DIM), row),
                  pl.BlockSpec((tm, 2 * D_MODEL), row),
                  _const_spec((CONV_CH, D_MODEL)), _const_spec((N_HEADS * HEAD_DIM, D_MODEL)),
                  _const_spec((D_MODEL, D_MODEL)), _const_spec((1, 1, D_MODEL)),
                  _const_spec((D_MODEL, LANES)), _const_spec((1, LANES))],
        out_specs=[pl.BlockSpec((bb, tt, D_MODEL), lambda i, j: (i, j, 0)),
                   pl.BlockSpec((tm, D_MODEL), row), pl.BlockSpec((tm, LANES), row)],
        out_shape=[jax.ShapeDtypeStruct((B, T, D_MODEL), F32), jax.ShapeDtypeStruct((n, D_MODEL), BF16),
                   jax.ShapeDtypeStruct((n, LANES), F32)],
        compiler_params=_cparams(("arbitrary", "arbitrary")),
        name="merge_router",
    )(x, g1, sc2, sh2, yc, attn, gates, w_conv_out.astype(BF16), w_attn_o.astype(BF16), w_out.astype(BF16),
      norm2_g.reshape(1, 1, D_MODEL), wr, br)


def _moe_kernel(h2_ref, gate_ref, w13_ref, w2_ref, x1_ref, g2_ref, o_ref, acc_ref):
    e = pl.program_id(2)
    bb, tt, _ = x1_ref.shape

    @pl.when(e == 0)
    def _():
        acc_ref[...] = jnp.zeros_like(acc_ref)

    ab = jnp.dot(h2_ref[...], w13_ref[0], preferred_element_type=F32)
    lane = lax.broadcasted_iota(jnp.int32, gate_ref.shape, 1)
    gcol = jnp.sum(jnp.where(lane == e, gate_ref[...], 0.0), axis=-1, keepdims=True)
    hid = _silu(ab[:, :EXPERT_FF]) * ab[:, EXPERT_FF:] * gcol
    acc_ref[...] += jnp.dot(hid.astype(BF16), w2_ref[0], preferred_element_type=F32)

    @pl.when(e == N_EXPERTS - 1)
    def _():
        o_ref[...] = x1_ref[...] + g2_ref[...] * acc_ref[...].reshape(bb, tt, D_MODEL)


def _moe(h2, gate, x1, g2, w13, w2b, bb, tt):
    B, T, _ = x1.shape
    tm = bb * tt
    nb_t = T // tt
    row = lambda i, j, e: (i * nb_t + j, 0)
    return pl.pallas_call(
        _moe_kernel,
        grid=(B // bb, nb_t, N_EXPERTS),
        in_specs=[pl.BlockSpec((tm, D_MODEL), row), pl.BlockSpec((tm, LANES), row),
                  pl.BlockSpec((1, D_MODEL, 2 * EXPERT_FF), lambda i, j, e: (e, 0, 0)),
                  pl.BlockSpec((1, EXPERT_FF, D_MODEL), lambda i, j, e: (e, 0, 0)),
                  pl.BlockSpec((bb, tt, D_MODEL), lambda i, j, e: (i, j, 0)),
                  pl.BlockSpec((bb, 1, D_MODEL), lambda i, j, e: (i, 0, 0))],
        out_specs=pl.BlockSpec((bb, tt, D_MODEL), lambda i, j, e: (i, j, 0)),
        out_shape=jax.ShapeDtypeStruct((B, T, D_MODEL), F32),
        scratch_shapes=[pltpu.VMEM((tm, D_MODEL), F32)],
        compiler_params=_cparams(("arbitrary", "arbitrary", "arbitrary")),
        name="moe_dense",
    )(h2, gate, w13, w2b, x1, g2)


def _lane_bcast(col):
    return jnp.broadcast_to(col, (col.shape[0], LANES))


def _row_any(flag_rows):
    return jnp.max(jnp.where(flag_rows, 1.0, 0.0)) > 0.0


def _select_rows(sc_ref, nch, cw, n_keep, jl_ref):
    R = sc_ref.shape[0]
    ng = cw // LANES
    kf = float(n_keep)
    inf = jnp.full((R, LANES), jnp.inf, F32)
    zero = jnp.zeros((R, LANES), F32)
    lane = lax.broadcasted_iota(jnp.int32, (R, LANES), 1).astype(F32)

    def scan(init, fn):
        def body(c, carry):
            base = pl.multiple_of(c * cw, LANES)
            for j in range(ng):
                blk = sc_ref[:, pl.ds(base + j * LANES, LANES)]
                carry = fn(carry, blk, (base + j * LANES).astype(F32))
            return carry
        return lax.fori_loop(0, nch, body, init)

    rsum = lambda a: _lane_bcast(jnp.sum(a, axis=-1, keepdims=True))
    rmin = lambda a: _lane_bcast(jnp.min(a, axis=-1, keepdims=True))
    rmax = lambda a: _lane_bcast(jnp.max(a, axis=-1, keepdims=True))

    mx, mn = scan((-inf, inf), lambda c, x, _: (jnp.maximum(c[0], x),
                                                 jnp.minimum(c[1], jnp.where(x == -jnp.inf, jnp.inf, x))))
    hi0, lo0 = rmax(mx), rmin(mn)

    def bisect(_, lh):
        lo, hi = lh
        mid = 0.5 * lo + 0.5 * hi
        cnt = rsum(scan(zero, lambda c, x, _: c + jnp.where(x >= mid, 1.0, 0.0)))
        ge = cnt >= kf
        return jnp.where(ge, mid, lo), jnp.where(ge, hi, mid)

    lo, _ = lax.fori_loop(0, N_BISECT, bisect, (lo0, hi0))
    tau0 = rmin(scan(inf, lambda c, x, _: jnp.minimum(c, jnp.where(x >= lo, x, jnp.inf))))

    def refine_pass(tau):
        def fn(c, x, _):
            gt = x > tau
            return (c[0] + jnp.where(gt, 1.0, 0.0), jnp.minimum(c[1], jnp.where(gt, x, jnp.inf)),
                    c[2] + jnp.where(x == tau, 1.0, 0.0))
        cgt, nxt, ceq = scan((zero, inf, zero), fn)
        return rsum(cgt), rmin(nxt), rsum(ceq)

    def refine_body(state):
        tau, _, _, _ = state
        cgt, nxt, ceq = refine_pass(tau)
        up = cgt >= kf
        return jnp.where(up, nxt, tau), cgt, ceq, _row_any(up)

    tau, cgt, ceq, _ = lax.while_loop(lambda s: s[3], refine_body, (tau0, zero, zero, jnp.bool_(True)))

    need = kf - cgt
    excess = ceq > need
    jl_ref[...] = inf

    @pl.when(_row_any(excess))
    def _():
        def idx_bisect(_, lh):
            lo_j, hi_j = lh
            mid = jnp.floor((lo_j + hi_j) * 0.5)
            cnt = rsum(scan(zero, lambda c, x, col0: c + jnp.where((x == tau) & (col0 + lane <= mid), 1.0, 0.0)))
            ge = cnt >= need
            return jnp.where(ge, lo_j, mid), jnp.where(ge, mid, hi_j)

        ncol = (nch * cw).astype(F32) if hasattr(nch, "astype") else float(nch * cw)
        _, hi_j = lax.fori_loop(0, 14, idx_bisect, (zero - 1.0, zero + (ncol - 1.0)))
        jl_ref[...] = jnp.where(excess, hi_j, jnp.inf)

    return tau


ATT_TQ = 128
ATT_TK = 512
IDX_TN = 256

_NT = (((1,), (1,)), ((), ()))


def _kv_lane_offset(head):
    return ((head // 2) % 2) * HEAD_DIM


def _assemble_heads(per_head):
    lane = lax.broadcasted_iota(jnp.int32, per_head[0].shape, 1)
    cols = []
    for c in range(N_HEADS // 2):
        a, b = per_head[2 * c], per_head[2 * c + 1]
        if _kv_lane_offset(2 * c) == 0:
            b = pltpu.roll(b, HEAD_DIM, 1)
        else:
            a = pltpu.roll(a, HEAD_DIM, 1)
        cols.append(jnp.where(lane < HEAD_DIM, a, b))
    return jnp.concatenate(cols, axis=1)


def _attn_prompt_kernel(q_ref, qi_ref, kiw_ref, kib_ref, kb_ref, vb_ref, o_ref,
                        sc_ref, wb_ref, jl_ref, m_ref, l_ref, acc_ref, *, n_keep):
    tq, tk = ATT_TQ, ATT_TK
    qb = pl.program_id(1)
    t0 = qb * tq
    nkb = (t0 + tq + tk - 1) // tk

    for h in range(IDX_HEADS):
        wb_ref[h] = _lane_bcast(kiw_ref[:, IDX_DIM + h:IDX_DIM + h + 1])

    row_g = t0 + lax.broadcasted_iota(jnp.int32, (tq, IDX_TN), 0)
    col_l = lax.broadcasted_iota(jnp.int32, (tq, IDX_TN), 1)

    def score_body(c, _):
        c0 = pl.multiple_of(c * IDX_TN, IDX_TN)
        kt = kib_ref[0, pl.ds(c0, IDX_TN), :]
        acc = jnp.zeros((tq, IDX_TN), F32)
        for h in range(IDX_HEADS):
            y = lax.dot_general(qi_ref[:, h * LANES:(h + 1) * LANES], kt, _NT, preferred_element_type=F32)
            wb = wb_ref[h]
            acc = acc + jnp.maximum(y, 0.0) * jnp.concatenate([wb] * (IDX_TN // LANES), axis=1)
        sc_ref[:, pl.ds(c0, IDX_TN)] = jnp.where(c0 + col_l <= row_g, acc * IDX_SCALE, -jnp.inf)
        return 0

    lax.fori_loop(0, nkb * (tk // IDX_TN), score_body, 0)

    tau = _select_rows(sc_ref, nkb, tk, n_keep, jl_ref)[:, 0:1]
    jl = jl_ref[:, 0:1]

    m_ref[...] = jnp.full(m_ref.shape, NEG_BIG, F32)
    l_ref[...] = jnp.zeros(l_ref.shape, F32)
    acc_ref[...] = jnp.zeros(acc_ref.shape, F32)
    col_k = lax.broadcasted_iota(jnp.int32, (tq, tk), 1).astype(F32)

    def attn_body(c, _):
        c0 = pl.multiple_of(c * tk, tk)
        x = sc_ref[:, pl.ds(c0, tk)]
        sel = (x > tau) | ((x == tau) & (col_k + c0.astype(F32) <= jl))
        for pair in range(N_KV_HEADS // 2):
            kt = kb_ref[0, pl.ds(c0, tk), pair * LANES:(pair + 1) * LANES]
            vt = vb_ref[0, pl.ds(c0, tk), pair * LANES:(pair + 1) * LANES]
            for hh in range(N_HEADS // 2):
                h = pair * (N_HEADS // 2) + hh
                s = lax.dot_general(q_ref[:, h * LANES:(h + 1) * LANES], kt, _NT, preferred_element_type=F32)
                s = jnp.where(sel, s, NEG_BIG)
                m_old = m_ref[h]
                m_new = jnp.maximum(m_old, _lane_bcast(jnp.max(s, axis=-1, keepdims=True)))
                alpha = jnp.exp(m_old - m_new)
                p = jnp.exp(s - m_new[:, 0:1])
                l_ref[h] = alpha * l_ref[h] + _lane_bcast(jnp.sum(p, axis=-1, keepdims=True))
                acc_ref[h] = alpha * acc_ref[h] + jnp.dot(p.astype(BF16), vt, preferred_element_type=F32)
                m_ref[h] = m_new
        return 0

    lax.fori_loop(0, nkb, attn_body, 0)
    o_ref[...] = _assemble_heads([acc_ref[h] / l_ref[h] for h in range(N_HEADS)]).astype(BF16)


def _attn_prompt(q, qi, kiw, kib, kb, vb, B, S):
    tq = ATT_TQ
    nq = S // tq
    n_keep = min(TOPK_MAX, S // 4)
    row = lambda b, j: (b * nq + j, 0)
    per_b = lambda b, j: (b, 0, 0)
    return pl.pallas_call(
        functools.partial(_attn_prompt_kernel, n_keep=n_keep),
        grid=(B, nq),
        in_specs=[pl.BlockSpec((tq, N_HEADS * LANES), row), pl.BlockSpec((tq, IDX_HEADS * LANES), row),
                  pl.BlockSpec((tq, LANES), row),
                  pl.BlockSpec((1, S, LANES), per_b), pl.BlockSpec((1, S, KV_W), per_b),
                  pl.BlockSpec((1, S, KV_W), per_b)],
        out_specs=pl.BlockSpec((tq, N_HEADS * HEAD_DIM), row),
        out_shape=jax.ShapeDtypeStruct((B * S, N_HEADS * HEAD_DIM), BF16),
        scratch_shapes=[pltpu.VMEM((tq, S), F32), pltpu.VMEM((IDX_HEADS, tq, LANES), F32),
                        pltpu.VMEM((tq, LANES), F32), pltpu.VMEM((N_HEADS, tq, LANES), F32),
                        pltpu.VMEM((N_HEADS, tq, LANES), F32), pltpu.VMEM((N_HEADS, tq, LANES), F32)],
        compiler_params=_cparams(("arbitrary", "arbitrary")),
        name="attn_prompt",
    )(q, qi, kiw, kib.reshape(B, S, LANES), kb.reshape(B, S, KV_W), vb.reshape(B, S, KV_W))


SCORE_PAGES = 16
ATTN_PAGES = 8
SEL_ROWS = 128
SEL_CW = 640


def _sample_scores_kernel(pt_ref, qi_ref, w_ref, kin_ref, *rest):
    del pt_ref
    page_refs, o_ref = rest[:SCORE_PAGES], rest[SCORE_PAGES]
    g = pl.program_id(1)
    t = o_ref.shape[1]
    qi = qi_ref[0]
    wb = _lane_bcast(w_ref[0])

    def page_scores(kpage):
        y = lax.dot_general(qi, kpage.astype(BF16), _NT, preferred_element_type=F32)
        z = jnp.maximum(y, 0.0) * wb
        tot = z[0:t]
        for h in range(1, IDX_HEADS):
            tot = tot + z[h * t:(h + 1) * t]
        return tot * IDX_SCALE

    for j in range(SCORE_PAGES):
        col0 = pl.multiple_of((g * SCORE_PAGES + j) * PAGE_SIZE, PAGE_SIZE)
        o_ref[0, :, pl.ds(col0, PAGE_SIZE)] = page_scores(page_refs[j][0])

    @pl.when(g == 0)
    def _():
        past = o_ref.shape[2] - PAGE_SIZE
        qpos = lax.broadcasted_iota(jnp.int32, (t, PAGE_SIZE), 0)
        kpos = lax.broadcasted_iota(jnp.int32, (t, PAGE_SIZE), 1)
        o_ref[0, :, past:] = jnp.where(kpos <= qpos, page_scores(kin_ref[0]), -jnp.inf)


def _sample_scores(page_table, qi_s, w_s, ki_new_pad, cache_kidx_l):
    B, n_pages = page_table.shape
    t = qi_s.shape[1] // IDX_HEADS
    past = n_pages * PAGE_SIZE
    per_b = lambda b, g, pt: (b, 0, 0)
    page_specs = [pl.BlockSpec((1, PAGE_SIZE, IDX_DIM),
                               functools.partial(lambda b, g, pt, j: (pt[b, g * SCORE_PAGES + j], 0, 0), j=j))
                  for j in range(SCORE_PAGES)]
    return pl.pallas_call(
        _sample_scores_kernel,
        grid_spec=pltpu.PrefetchScalarGridSpec(
            num_scalar_prefetch=1,
            grid=(B, n_pages // SCORE_PAGES),
            in_specs=[pl.BlockSpec((1, IDX_HEADS * t, IDX_DIM), per_b), pl.BlockSpec((1, IDX_HEADS * t, 1), per_b),
                      pl.BlockSpec((1, PAGE_SIZE, IDX_DIM), per_b)] + page_specs,
            out_specs=pl.BlockSpec((1, t, past + PAGE_SIZE), per_b)),
        out_shape=jax.ShapeDtypeStruct((B, t, past + PAGE_SIZE), F32),
        compiler_params=_cparams(("arbitrary", "arbitrary")),
        name="sample_scores",
    )(page_table, qi_s, w_s, ki_new_pad, *([cache_kidx_l] * SCORE_PAGES))


def _select_kernel(sc_ref, tau_ref, jl_ref, *, n_keep):
    tau_ref[...] = _select_rows(sc_ref, sc_ref.shape[1] // SEL_CW, SEL_CW, n_keep, jl_ref)


def _sample_select(scores2d, n_keep):
    r, l = scores2d.shape
    rows = min(SEL_ROWS, r)
    row = lambda i: (i, 0)
    return pl.pallas_call(
        functools.partial(_select_kernel, n_keep=n_keep),
        grid=(r // rows,),
        in_specs=[pl.BlockSpec((rows, l), row)],
        out_specs=[pl.BlockSpec((rows, LANES), row), pl.BlockSpec((rows, LANES), row)],
        out_shape=[jax.ShapeDtypeStruct((r, LANES), F32), jax.ShapeDtypeStruct((r, LANES), F32)],
        compiler_params=_cparams(("arbitrary",)),
        name="sample_select",
    )(scores2d)


def _sample_attn_kernel(pt_ref, q_ref, sc_ref, tau_ref, jl_ref, kn_ref, vn_ref, *rest):
    del pt_ref
    k_refs, v_refs = rest[:ATTN_PAGES], rest[ATTN_PAGES:2 * ATTN_PAGES]
    o_ref, m_ref, l_ref, acc_ref = rest[2 * ATTN_PAGES:]
    g = pl.program_id(1)
    t = sc_ref.shape[1]
    half = N_HEADS // 2 * t

    @pl.when(g == 0)
    def _():
        m_ref[...] = jnp.full(m_ref.shape, NEG_BIG, F32)
        l_ref[...] = jnp.zeros(l_ref.shape, F32)
        acc_ref[...] = jnp.zeros(acc_ref.shape, F32)

    tau = tau_ref[0][:, 0:1]
    jl = jl_ref[0][:, 0:1]
    lane = lax.broadcasted_iota(jnp.int32, (t, PAGE_SIZE), 1).astype(F32)

    def page(kp, vp, col0):
        x = sc_ref[0, :, pl.ds(col0, PAGE_SIZE)]
        col0f = float(col0) if isinstance(col0, int) else col0.astype(F32)
        sel = (x > tau) | ((x == tau) & (lane + col0f <= jl))
        sel = jnp.concatenate([jnp.where(sel, 1.0, 0.0)] * (N_HEADS // 2), axis=0) > 0.5
        for pair in range(N_KV_HEADS // 2):
            rows = slice(pair * half, (pair + 1) * half)
            s = lax.dot_general(q_ref[0, rows, :], kp[:, pair * LANES:(pair + 1) * LANES], _NT,
                                preferred_element_type=F32)
            s = jnp.where(sel, s, NEG_BIG)
            m_old = m_ref[rows]
            m_new = jnp.maximum(m_old, _lane_bcast(jnp.max(s, axis=-1, keepdims=True)))
            alpha = jnp.exp(m_old - m_new)
            p = jnp.exp(s - m_new)
            l_ref[rows] = alpha * l_ref[rows] + _lane_bcast(jnp.sum(p, axis=-1, keepdims=True))
            acc_ref[rows] = alpha * acc_ref[rows] + jnp.dot(p.astype(BF16), vp[:, pair * LANES:(pair + 1) * LANES],
                                                            preferred_element_type=F32)
            m_ref[rows] = m_new

    for j in range(ATTN_PAGES):
        col0 = pl.multiple_of((g * ATTN_PAGES + j) * PAGE_SIZE, PAGE_SIZE)
        page(k_refs[j][0].astype(BF16), v_refs[j][0].astype(BF16), col0)

    @pl.when(g == pl.num_programs(1) - 1)
    def _():
        page(kn_ref[0], vn_ref[0], sc_ref.shape[2] - PAGE_SIZE)
        o_ref[0] = acc_ref[...] / l_ref[...]


def _sample_attn(page_table, q_s, scores, tau, jl, k_new_pad, v_new_pad, cache_k_l, cache_v_l):
    B, n_pages = page_table.shape
    t = scores.shape[1]
    rows = N_HEADS * t
    per_b = lambda b, g, pt: (b, 0, 0)

    def page_specs():
        return [pl.BlockSpec((1, PAGE_SIZE, KV_W),
                             functools.partial(lambda b, g, pt, j: (pt[b, g * ATTN_PAGES + j], 0, 0), j=j))
                for j in range(ATTN_PAGES)]

    return pl.pallas_call(
        _sample_attn_kernel,
        grid_spec=pltpu.PrefetchScalarGridSpec(
            num_scalar_prefetch=1,
            grid=(B, n_pages // ATTN_PAGES),
            in_specs=[pl.BlockSpec((1, rows, LANES), per_b), pl.BlockSpec((1, t, scores.shape[2]), per_b),
                      pl.BlockSpec((1, t, LANES), per_b), pl.BlockSpec((1, t, LANES), per_b),
                      pl.BlockSpec((1, PAGE_SIZE, KV_W), per_b), pl.BlockSpec((1, PAGE_SIZE, KV_W), per_b)]
                     + page_specs() + page_specs(),
            out_specs=pl.BlockSpec((1, rows, LANES), per_b),
            scratch_shapes=[pltpu.VMEM((rows, LANES), F32)] * 3),
        out_shape=jax.ShapeDtypeStruct((B, rows, LANES), F32),
        compiler_params=_cparams(("arbitrary", "arbitrary")),
        name="sample_attn",
    )(page_table, q_s, scores, tau, jl, k_new_pad, v_new_pad,
      *([cache_k_l] * ATTN_PAGES), *([cache_v_l] * ATTN_PAGES))


def _sample_attention(sr, B, T, page_table, cache_k_l, cache_v_l, cache_kidx_l):
    n_pages = page_table.shape[1]
    n_keep = min(TOPK_MAX, (n_pages * PAGE_SIZE + T) // 4)
    n_pool = cache_k_l.shape[0]

    def head_major(a, width):
        a = a.reshape(B, T, N_HEADS, LANES)[..., :width]
        return jnp.transpose(a, (0, 2, 1, 3)).reshape(B, N_HEADS * T, width)

    def pad_page(a):
        return jnp.pad(a.reshape(B, T, -1), ((0, 0), (0, PAGE_SIZE - T), (0, 0)))

    qi_s = head_major(sr["qi"], IDX_DIM)
    w_s = jnp.transpose(sr["kiw"][:, IDX_DIM:IDX_DIM + IDX_HEADS].reshape(B, T, IDX_HEADS), (0, 2, 1))
    w_s = w_s.reshape(B, IDX_HEADS * T, 1)
    scores = _sample_scores(page_table, qi_s, w_s, pad_page(sr["kiw"][:, :IDX_DIM]), cache_kidx_l)
    tau, jl = _sample_select(scores.reshape(B * T, -1), n_keep)
    o = _sample_attn(page_table, head_major(sr["q"], LANES), scores, tau.reshape(B, T, LANES),
                     jl.reshape(B, T, LANES), pad_page(sr["kb"]), pad_page(sr["vb"]),
                     cache_k_l.reshape(n_pool, PAGE_SIZE, KV_W), cache_v_l.reshape(n_pool, PAGE_SIZE, KV_W))
    o = o.reshape(B, N_HEADS, T, LANES)
    heads = [o[:, h, :, _kv_lane_offset(h):_kv_lane_offset(h) + HEAD_DIM] for h in range(N_HEADS)]
    return jnp.stack(heads, axis=2).reshape(B * T, N_HEADS * HEAD_DIM).astype(BF16)


def _layer(x, mods, conv_past, attn_fn, p, tiles):
    B, T, _ = x.shape
    sh1, sc1, g1, sh2, sc2, g2 = mods
    pr = _inproj(x, sc1, sh1, p["norm1_g"], p["w_pack"], p["q_norm_g"], p["k_norm_g"], *tiles["inproj"])
    cin = pr["cin"].reshape(B, T, CONV_CH)
    if conv_past is None:
        yc = _conv_prompt(pr["cin"], B, T, p["conv_w"], p["conv_b"], p["ln_g"], p["ln_b"])
        conv_state = cin[:, T - (CONV_W - 1):]
    else:
        cpad = jnp.concatenate([conv_past, cin], axis=1)
        yc = _conv_sample(cpad, p["conv_w"], p["conv_b"], p["ln_g"], p["ln_b"]).reshape(B * T, CONV_CH)
        conv_state = cpad[:, T:]
    attn = attn_fn(pr)
    x1, h2, gate = _merge(x, g1, sc2, sh2, yc, attn, pr["gates"], p["w_conv_out"], p["w_attn_o"], p["w_out"],
                          p["norm2_g"], p["w_grp"], p["b_grp"], p["w_exp"], p["b_exp"], *tiles["merge"])
    y = _moe(h2, gate, x1, g2, p["w13"], p["w2b"], *tiles["moe"])
    k = pr["k"].reshape(B, T, N_KV_HEADS, HEAD_DIM)
    v = pr["v"].reshape(B, T, N_KV_HEADS, HEAD_DIM)
    ki = pr["kiw"][:, :IDX_DIM].reshape(B, T, IDX_DIM)
    return y, k, v, ki, conv_state


def kernel(x_prompt, x_sample, cache_k, cache_v, cache_kidx, state_conv, page_table, c_prompt, c_sample,
           w_ada, b_ada, norm1_g, w_in, q_norm_g, k_norm_g, conv_w, conv_b, ln_g, ln_b, w_conv_out, w_attn_o,
           w_out, norm2_g, w_grp, b_grp, w_exp, b_exp, w1, w3, w2):
    depth = w_ada.shape[0]
    Bp, S, _ = x_prompt.shape
    Bs, T, _ = x_sample.shape
    xp, xs = x_prompt, x_sample
    outs = [[] for _ in range(8)]
    c_all = jnp.concatenate([c_prompt, c_sample], axis=0)
    c_all = jnp.pad(c_all, ((0, (-c_all.shape[0]) % 16), (0, 0)))
    tiles_p = {"inproj": (1, min(256, S)), "merge": (1, min(512, S)), "moe": (1, min(1024, S))}
    tiles_s = {"inproj": (min(32, Bs), T), "merge": (min(64, Bs), T), "moe": (min(128, Bs), T)}
    for l in range(depth):
        p = dict(norm1_g=norm1_g[l], w_pack=_pack_w_in(w_in[l]), q_norm_g=q_norm_g[l], k_norm_g=k_norm_g[l],
                 conv_w=conv_w[l], conv_b=conv_b[l], ln_g=ln_g[l], ln_b=ln_b[l], w_conv_out=w_conv_out[l],
                 w_attn_o=w_attn_o[l], w_out=w_out[l], norm2_g=norm2_g[l], w_grp=w_grp[l], b_grp=b_grp[l],
                 w_exp=w_exp[l], b_exp=b_exp[l],
                 w13=jnp.concatenate([w1[l], w3[l]], axis=-1).astype(BF16), w2b=w2[l].astype(BF16))
        mod = _ada(c_all, w_ada[l], b_ada[l])
        mods = [m[:, None, :] for m in jnp.split(mod, 6, axis=-1)]
        mods_p = [m[:Bp] for m in mods]
        mods_s = [m[Bp:Bp + Bs] for m in mods]

        attn_p = lambda pr: _attn_prompt(pr["q"], pr["qi"], pr["kiw"], pr["kib"], pr["kb"], pr["vb"], Bp, S)
        xp, k, v, ki, cv = _layer(xp, mods_p, None, attn_p, p, tiles_p)
        for o, a in zip(outs[:4], (k, v, ki, cv)):
            o.append(a)

        attn_s = lambda pr: _sample_attention(pr, Bs, T, page_table, cache_k[l], cache_v[l], cache_kidx[l])
        xs, k, v, ki, cv = _layer(xs, mods_s, state_conv[l], attn_s, p, tiles_s)
        for o, a in zip(outs[4:], (k, v, ki, cv)):
            o.append(a)
    return (xp, xs) + tuple(jnp.stack(o) for o in outs)
```

```python
import functools

import jax
import jax.numpy as jnp
from jax import lax
from jax.experimental import pallas as pl
from jax.experimental.pallas import tpu as pltpu

F32 = jnp.float32
BF16 = jnp.bfloat16

D_MODEL = 1024
CONV_CH = 512
CONV_W = 31
N_HEADS = 8
HEAD_DIM = 64
N_KV_HEADS = 4
KV_W = N_KV_HEADS * HEAD_DIM
IDX_HEADS = 8
IDX_DIM = 64
IDX_SCALE = (IDX_HEADS * IDX_DIM) ** -0.5
TOPK_MAX = 256
PAGE_SIZE = 128
N_GROUPS = 4
EXP_PER_GROUP = 8
N_EXPERTS = 32
EXPERT_FF = 256
EPS = 1e-6

LANES = 128
NEG_BIG = -1e30
VMEM_LIMIT = 56 * 1024 * 1024

C_GLU = 0
C_Q = C_GLU + 2 * CONV_CH
C_K = C_Q + N_HEADS * LANES
C_V = C_K + KV_W
C_QI = C_V + KV_W
C_KIW = C_QI + IDX_HEADS * LANES
C_GATES = C_KIW + LANES
IN_PACK = C_GATES + 2 * D_MODEL

N_BISECT = 18


def _cparams(sem):
    return pltpu.CompilerParams(dimension_semantics=sem, vmem_limit_bytes=VMEM_LIMIT)


def _sigmoid(x):
    return 1.0 / (1.0 + jnp.exp(-x))


def _silu(x):
    return x * _sigmoid(x)


def _const_spec(shape):
    nd = len(shape)
    return pl.BlockSpec(shape, lambda *_: (0,) * nd)


def _ada_kernel(c_ref, w_ref, b_ref, o_ref):
    s = _silu(c_ref[...])
    o_ref[...] = jnp.dot(s.astype(BF16), w_ref[...], preferred_element_type=F32) + b_ref[...]


def _ada(c, w_ada, b_ada):
    r = c.shape[0]
    tn = 1536
    return pl.pallas_call(
        _ada_kernel,
        grid=(6 * D_MODEL // tn,),
        in_specs=[pl.BlockSpec((r, D_MODEL), lambda j: (0, 0)),
                  pl.BlockSpec((D_MODEL, tn), lambda j: (0, j)),
                  pl.BlockSpec((1, tn), lambda j: (0, j))],
        out_specs=pl.BlockSpec((r, tn), lambda j: (0, j)),
        out_shape=jax.ShapeDtypeStruct((r, 6 * D_MODEL), F32),
        compiler_params=_cparams(("arbitrary",)),
        name="adaln_mod",
    )(c, w_ada.astype(BF16), b_ada.reshape(1, -1))


def _inproj_kernel(x_ref, sc_ref, sh_ref, g_ref, w_ref, qg_ref, kg_ref, ones_ref,
                   cin_ref, q_ref, k_ref, v_ref, kb_ref, vb_ref, qi_ref, kiw_ref, kib_ref, gates_ref):
    bb, tt, _ = x_ref.shape
    x = x_ref[...]
    ms = jnp.mean(x * x, axis=-1, keepdims=True)
    h = x * lax.rsqrt(ms + EPS) * g_ref[...] * (1.0 + sc_ref[...]) + sh_ref[...]
    hb = h.reshape(bb * tt, D_MODEL).astype(BF16)

    def proj(c0, width):
        return jnp.dot(hb, w_ref[:, c0:c0 + width], preferred_element_type=F32)

    glu = proj(C_GLU, 2 * CONV_CH)
    cin_ref[...] = glu[:, :CONV_CH] * _sigmoid(glu[:, CONV_CH:])

    for hd in range(N_HEADS):
        qh = proj(C_Q + hd * LANES, LANES)
        msq = jnp.sum(qh * qh, axis=-1, keepdims=True) * (1.0 / HEAD_DIM)
        qn = qh * lax.rsqrt(msq + EPS) * qg_ref[:, hd * LANES:(hd + 1) * LANES]
        q_ref[:, hd * LANES:(hd + 1) * LANES] = (qn * (HEAD_DIM ** -0.5)).astype(BF16)

    k = proj(C_K, KV_W)
    k2 = k * k
    k2_hi = k2.astype(BF16)
    k2_lo = (k2 - k2_hi.astype(F32)).astype(BF16)
    ksum = (jnp.dot(k2_hi, ones_ref[...], preferred_element_type=F32)
            + jnp.dot(k2_lo, ones_ref[...], preferred_element_type=F32))
    kn = k * lax.rsqrt(ksum * (1.0 / HEAD_DIM) + EPS) * kg_ref[...]
    k_ref[...] = kn
    kb_ref[...] = kn.astype(BF16)

    v = proj(C_V, KV_W)
    v_ref[...] = v
    ones_blk = jnp.ones((v.shape[0], LANES), BF16)
    for pair in range(N_KV_HEADS // 2):
        vb_ref[:, pair * 2 * LANES:pair * 2 * LANES + LANES] = v[:, pair * LANES:(pair + 1) * LANES].astype(BF16)
        vb_ref[:, pair * 2 * LANES + LANES:(pair + 1) * 2 * LANES] = ones_blk

    qi_ref[...] = proj(C_QI, IDX_HEADS * LANES).astype(BF16)

    kiw = proj(C_KIW, LANES)
    kiw_ref[...] = kiw
    lane = lax.broadcasted_iota(jnp.int32, kiw.shape, 1)
    kib_ref[...] = jnp.where(lane < IDX_DIM, kiw, 0.0).astype(BF16)

    gates_ref[...] = _sigmoid(proj(C_GATES, 2 * D_MODEL))


def _pack_w_in(w_in):
    o = 0
    segs = {}
    for name, width in (("glu", 2 * CONV_CH), ("q", N_HEADS * HEAD_DIM), ("k", KV_W), ("v", KV_W),
                        ("qi", IDX_HEADS * IDX_DIM), ("ki", IDX_DIM), ("wi", IDX_HEADS), ("gates", 2 * D_MODEL)):
        segs[name] = w_in[:, o:o + width]
        o += width
    zeros64 = jnp.zeros((D_MODEL, HEAD_DIM), F32)
    cols = [segs["glu"]]
    for hd in range(N_HEADS):
        wq = segs["q"][:, hd * HEAD_DIM:(hd + 1) * HEAD_DIM]
        cols += [zeros64, wq] if (hd // 2) % 2 else [wq, zeros64]
    cols += [segs["k"], segs["v"]]
    for hd in range(IDX_HEADS):
        cols += [segs["qi"][:, hd * IDX_DIM:(hd + 1) * IDX_DIM], zeros64]
    cols += [segs["ki"], segs["wi"], jnp.zeros((D_MODEL, LANES - IDX_DIM - IDX_HEADS), F32)]
    cols += [segs["gates"]]
    return jnp.concatenate(cols, axis=1).astype(BF16)


def _pack_q_gain(q_norm_g):
    z = jnp.zeros((HEAD_DIM,), F32)
    parts = []
    for hd in range(N_HEADS):
        parts += [z, q_norm_g] if (hd // 2) % 2 else [q_norm_g, z]
    return jnp.concatenate(parts).reshape(1, N_HEADS * LANES)


def _inproj(x, sc1, sh1, norm1_g, w_pack, q_norm_g, k_norm_g, bb, tt):
    B, T, _ = x.shape
    n = B * T
    tm = bb * tt
    nb_t = T // tt
    grid = (B // bb, nb_t)
    row = lambda i, j: (i * nb_t + j, 0)
    ones = (lax.broadcasted_iota(jnp.int32, (KV_W, KV_W), 0) // HEAD_DIM
            == lax.broadcasted_iota(jnp.int32, (KV_W, KV_W), 1) // HEAD_DIM).astype(BF16)
    outs = [("cin", CONV_CH, F32), ("q", N_HEADS * LANES, BF16), ("k", KV_W, F32), ("v", KV_W, F32),
            ("kb", KV_W, BF16), ("vb", 2 * KV_W, BF16), ("qi", IDX_HEADS * LANES, BF16), ("kiw", LANES, F32),
            ("kib", LANES, BF16), ("gates", 2 * D_MODEL, F32)]
    res = pl.pallas_call(
        _inproj_kernel,
        grid=grid,
        in_specs=[pl.BlockSpec((bb, tt, D_MODEL), lambda i, j: (i, j, 0)),
                  pl.BlockSpec((bb, 1, D_MODEL), lambda i, j: (i, 0, 0)),
                  pl.BlockSpec((bb, 1, D_MODEL), lambda i, j: (i, 0, 0)),
                  _const_spec((1, 1, D_MODEL)),
                  _const_spec((D_MODEL, IN_PACK)),
                  _const_spec((1, N_HEADS * LANES)),
                  _const_spec((1, KV_W)),
                  _const_spec((KV_W, KV_W))],
        out_specs=[pl.BlockSpec((tm, w), row) for _, w, _ in outs],
        out_shape=[jax.ShapeDtypeStruct((n, w), dt) for _, w, dt in outs],
        compiler_params=_cparams(("arbitrary", "arbitrary")),
        name="in_proj",
    )(x, sc1, sh1, norm1_g.reshape(1, 1, D_MODEL), w_pack, _pack_q_gain(q_norm_g),
      jnp.tile(k_norm_g, N_KV_HEADS).reshape(1, KV_W), ones)
    return {name: r for (name, _, _), r in zip(outs, res)}


CONV_HALO = 32
CONV_ROWS = 32


def _ln_silu(y, g, b):
    mu = jnp.mean(y, axis=-1, keepdims=True)
    yc = y - mu
    var = jnp.mean(yc * yc, axis=-1, keepdims=True)
    return _silu(yc * lax.rsqrt(var + EPS) * g + b)


def _conv_prompt_kernel(cin_ref, w_ref, b_ref, g_ref, lb_ref, o_ref, win_ref):
    tc = cin_ref.shape[0]

    @pl.when(pl.program_id(1) == 0)
    def _():
        win_ref[0:CONV_HALO, :] = jnp.zeros((CONV_HALO, CONV_CH), F32)

    win_ref[CONV_HALO:, :] = cin_ref[...]
    off = CONV_HALO - (CONV_W - 1)
    for r0 in range(0, tc, CONV_ROWS):
        acc = jnp.zeros((CONV_ROWS, CONV_CH), F32) + b_ref[...]
        for j in range(CONV_W):
            acc = acc + win_ref[r0 + off + j:r0 + off + j + CONV_ROWS, :] * w_ref[j:j + 1, :]
        o_ref[r0:r0 + CONV_ROWS, :] = _ln_silu(acc, g_ref[...], lb_ref[...]).astype(BF16)
    win_ref[0:CONV_HALO, :] = cin_ref[tc - CONV_HALO:, :]


def _conv_prompt(cin, B, S, conv_w, conv_b, ln_g, ln_b):
    tc = 256
    nt = S // tc
    return pl.pallas_call(
        _conv_prompt_kernel,
        grid=(B, nt),
        in_specs=[pl.BlockSpec((tc, CONV_CH), lambda b, j: (b * nt + j, 0)),
                  _const_spec((CONV_W, CONV_CH)), _const_spec((1, CONV_CH)),
                  _const_spec((1, CONV_CH)), _const_spec((1, CONV_CH))],
        out_specs=pl.BlockSpec((tc, CONV_CH), lambda b, j: (b * nt + j, 0)),
        out_shape=jax.ShapeDtypeStruct((B * S, CONV_CH), BF16),
        scratch_shapes=[pltpu.VMEM((CONV_HALO + tc, CONV_CH), F32)],
        compiler_params=_cparams(("arbitrary", "arbitrary")),
        name="conv_prompt",
    )(cin, conv_w, conv_b.reshape(1, -1), ln_g.reshape(1, -1), ln_b.reshape(1, -1))


def _conv_sample_kernel(cpad_ref, w_ref, b_ref, g_ref, lb_ref, o_ref):
    bb, tp, _ = cpad_ref.shape
    t = tp - (CONV_W - 1)
    for b in range(bb):
        acc = jnp.zeros((t, CONV_CH), F32) + b_ref[...]
        for j in range(CONV_W):
            acc = acc + cpad_ref[b, j:j + t, :] * w_ref[j:j + 1, :]
        o_ref[b] = _ln_silu(acc, g_ref[...], lb_ref[...]).astype(BF16)


def _conv_sample(cpad, conv_w, conv_b, ln_g, ln_b):
    B, tp, _ = cpad.shape
    t = tp - (CONV_W - 1)
    bb = 8
    return pl.pallas_call(
        _conv_sample_kernel,
        grid=(B // bb,),
        in_specs=[pl.BlockSpec((bb, tp, CONV_CH), lambda i: (i, 0, 0)),
                  _const_spec((CONV_W, CONV_CH)), _const_spec((1, CONV_CH)),
                  _const_spec((1, CONV_CH)), _const_spec((1, CONV_CH))],
        out_specs=pl.BlockSpec((bb, t, CONV_CH), lambda i: (i, 0, 0)),
        out_shape=jax.ShapeDtypeStruct((B, t, CONV_CH), BF16),
        compiler_params=_cparams(("arbitrary",)),
        name="conv_sample",
    )(cpad, conv_w, conv_b.reshape(1, -1), ln_g.reshape(1, -1), ln_b.reshape(1, -1))


R_EXP = 0
R_GRP = N_EXPERTS


def _lane_min_index(mask, lane):
    return jnp.min(jnp.where(mask, lane, float(LANES)), axis=-1, keepdims=True)


def _router_gate(logits):
    lane = lax.broadcasted_iota(jnp.int32, logits.shape, 1).astype(F32)
    is_grp = (lane >= R_GRP) & (lane < R_GRP + N_GROUPS)
    gl = jnp.where(is_grp, logits, -jnp.inf)
    gmax = jnp.max(gl, axis=-1, keepdims=True)
    gexp = jnp.where(is_grp, jnp.exp(gl - gmax), 0.0)
    p_grp = gexp / jnp.sum(gexp, axis=-1, keepdims=True)
    g_top = jnp.max(p_grp, axis=-1, keepdims=True)
    g_idx = _lane_min_index(is_grp & (p_grp == g_top), lane) - R_GRP
    in_grp = (lane >= g_idx * EXP_PER_GROUP) & (lane < (g_idx + 1) * EXP_PER_GROUP)
    el = jnp.where(in_grp, logits, -jnp.inf)
    emax = jnp.max(el, axis=-1, keepdims=True)
    eexp = jnp.where(in_grp, jnp.exp(el - emax), 0.0)
    p_exp = eexp / jnp.sum(eexp, axis=-1, keepdims=True)
    p1 = jnp.max(jnp.where(in_grp, p_exp, -1.0), axis=-1, keepdims=True)
    i1 = _lane_min_index(in_grp & (p_exp == p1), lane)
    rest = in_grp & (lane != i1)
    p2 = jnp.max(jnp.where(rest, p_exp, -1.0), axis=-1, keepdims=True)
    i2 = _lane_min_index(rest & (p_exp == p2), lane)
    tot = p1 + p2
    return jnp.where(lane == i1, g_top * p1 / tot, 0.0) + jnp.where(lane == i2, g_top * p2 / tot, 0.0)


def _merge_kernel(x_ref, g1_ref, sc2_ref, sh2_ref, yc_ref, at_ref, gates_ref, wco_ref, wao_ref, wout_ref,
                  n2g_ref, wr_ref, br_ref, x1_ref, h2_ref, gate_ref):
    bb, tt, _ = x_ref.shape
    y_conv = jnp.dot(yc_ref[...], wco_ref[...], preferred_element_type=F32)
    y_attn = jnp.dot(at_ref[...], wao_ref[...], preferred_element_type=F32)
    m = gates_ref[:, :D_MODEL] * y_conv + gates_ref[:, D_MODEL:] * y_attn
    mo = jnp.dot(m.astype(BF16), wout_ref[...], preferred_element_type=F32)
    x1 = x_ref[...] + g1_ref[...] * mo.reshape(bb, tt, D_MODEL)
    x1_ref[...] = x1
    ms = jnp.mean(x1 * x1, axis=-1, keepdims=True)
    h2 = (x1 * lax.rsqrt(ms + EPS) * n2g_ref[...] * (1.0 + sc2_ref[...]) + sh2_ref[...]).reshape(bb * tt, D_MODEL)
    h2_ref[...] = h2.astype(BF16)
    logits = jnp.dot(h2, wr_ref[...], preferred_element_type=F32, precision=lax.Precision.HIGHEST) + br_ref[...]
    gate_ref[...] = _router_gate(logits)


def _merge(x, g1, sc2, sh2, yc, attn, gates, w_conv_out, w_attn_o, w_out, norm2_g, w_grp, b_grp, w_exp, b_exp,
           bb, tt):
    B, T, _ = x.shape
    n = B * T
    tm = bb * tt
    nb_t = T // tt
    row = lambda i, j: (i * nb_t + j, 0)
    mod = lambda i, j: (i, 0, 0)
    pad = jnp.zeros((D_MODEL, LANES - N_EXPERTS - N_GROUPS), F32)
    wr = jnp.concatenate([w_exp, w_grp, pad], axis=1)
    br = jnp.concatenate([b_exp, b_grp, jnp.zeros((LANES - N_EXPERTS - N_GROUPS,), F32)]).reshape(1, LANES)
    return pl.pallas_call(
        _merge_kernel,
        grid=(B // bb, nb_t),
        in_specs=[pl.BlockSpec((bb, tt, D_MODEL), lambda i, j: (i, j, 0)),
                  pl.BlockSpec((bb, 1, D_MODEL), mod), pl.BlockSpec((bb, 1, D_MODEL), mod),
                  pl.BlockSpec((bb, 1, D_MODEL), mod),
                  pl.BlockSpec((tm, CONV_CH), row), pl.BlockSpec((tm, N_HEADS * HEAD_DIM), row),
                  pl.BlockSpec((tm, 2 * D_MODEL), row),
                  _const_spec((CONV_CH, D_MODEL)), _const_spec((N_HEADS * HEAD_DIM, D_MODEL)),
                  _const_spec((D_MODEL, D_MODEL)), _const_spec((1, 1, D_MODEL)),
                  _const_spec((D_MODEL, LANES)), _const_spec((1, LANES))],
        out_specs=[pl.BlockSpec((bb, tt, D_MODEL), lambda i, j: (i, j, 0)),
                   pl.BlockSpec((tm, D_MODEL), row), pl.BlockSpec((tm, LANES), row)],
        out_shape=[jax.ShapeDtypeStruct((B, T, D_MODEL), F32), jax.ShapeDtypeStruct((n, D_MODEL), BF16),
                   jax.ShapeDtypeStruct((n, LANES), F32)],
        compiler_params=_cparams(("arbitrary", "arbitrary")),
        name="merge_router",
    )(x, g1, sc2, sh2, yc, attn, gates, w_conv_out.astype(BF16), w_attn_o.astype(BF16), w_out.astype(BF16),
      norm2_g.reshape(1, 1, D_MODEL), wr, br)


def _moe_kernel(h2_ref, gate_ref, w13_ref, w2_ref, x1_ref, g2_ref, o_ref, acc_ref):
    e = pl.program_id(2)
    bb, tt, _ = x1_ref.shape

    @pl.when(e == 0)
    def _():
        acc_ref[...] = jnp.zeros_like(acc_ref)

    ab = jnp.dot(h2_ref[...], w13_ref[0], preferred_element_type=F32)
    lane = lax.broadcasted_iota(jnp.int32, gate_ref.shape, 1)
    gcol = jnp.sum(jnp.where(lane == e, gate_ref[...], 0.0), axis=-1, keepdims=True)
    hid = _silu(ab[:, :EXPERT_FF]) * ab[:, EXPERT_FF:] * gcol
    acc_ref[...] += jnp.dot(hid.astype(BF16), w2_ref[0], preferred_element_type=F32)

    @pl.when(e == N_EXPERTS - 1)
    def _():
        o_ref[...] = x1_ref[...] + g2_ref[...] * acc_ref[...].reshape(bb, tt, D_MODEL)


def _moe(h2, gate, x1, g2, w13, w2b, bb, tt):
    B, T, _ = x1.shape
    tm = bb * tt
    nb_t = T // tt
    row = lambda i, j, e: (i * nb_t + j, 0)
    return pl.pallas_call(
        _moe_kernel,
        grid=(B // bb, nb_t, N_EXPERTS),
        in_specs=[pl.BlockSpec((tm, D_MODEL), row), pl.BlockSpec((tm, LANES), row),
                  pl.BlockSpec((1, D_MODEL, 2 * EXPERT_FF), lambda i, j, e: (e, 0, 0)),
                  pl.BlockSpec((1, EXPERT_FF, D_MODEL), lambda i, j, e: (e, 0, 0)),
                  pl.BlockSpec((bb, tt, D_MODEL), lambda i, j, e: (i, j, 0)),
                  pl.BlockSpec((bb, 1, D_MODEL), lambda i, j, e: (i, 0, 0))],
        out_specs=pl.BlockSpec((bb, tt, D_MODEL), lambda i, j, e: (i, j, 0)),
        out_shape=jax.ShapeDtypeStruct((B, T, D_MODEL), F32),
        scratch_shapes=[pltpu.VMEM((tm, D_MODEL), F32)],
        compiler_params=_cparams(("arbitrary", "arbitrary", "arbitrary")),
        name="moe_dense",
    )(h2, gate, w13, w2b, x1, g2)


def _lane_bcast(col):
    return jnp.broadcast_to(col, (col.shape[0], LANES))


def _row_any(flag_rows):
    return jnp.max(jnp.where(flag_rows, 1.0, 0.0)) > 0.0


def _select_rows(sc_ref, nch, cw, n_keep, jl_ref):
    R = sc_ref.shape[0]
    ng = cw // LANES
    kf = float(n_keep)
    inf = jnp.full((R, LANES), jnp.inf, F32)
    zero = jnp.zeros((R, LANES), F32)
    lane = lax.broadcasted_iota(jnp.int32, (R, LANES), 1).astype(F32)

    def scan(init, fn):
        def body(c, carry):
            base = pl.multiple_of(c * cw, LANES)
            for j in range(ng):
                blk = sc_ref[:, pl.ds(base + j * LANES, LANES)]
                carry = fn(carry, blk, (base + j * LANES).astype(F32))
            return carry
        return lax.fori_loop(0, nch, body, init)

    rsum = lambda a: _lane_bcast(jnp.sum(a, axis=-1, keepdims=True))
    rmin = lambda a: _lane_bcast(jnp.min(a, axis=-1, keepdims=True))
    rmax = lambda a: _lane_bcast(jnp.max(a, axis=-1, keepdims=True))

    mx, mn = scan((-inf, inf), lambda c, x, _: (jnp.maximum(c[0], x),
                                                 jnp.minimum(c[1], jnp.where(x == -jnp.inf, jnp.inf, x))))
    hi0, lo0 = rmax(mx), rmin(mn)

    def bisect(_, lh):
        lo, hi = lh
        mid = 0.5 * lo + 0.5 * hi
        cnt = rsum(scan(zero, lambda c, x, _: c + jnp.where(x >= mid, 1.0, 0.0)))
        ge = cnt >= kf
        return jnp.where(ge, mid, lo), jnp.where(ge, hi, mid)

    lo, _ = lax.fori_loop(0, N_BISECT, bisect, (lo0, hi0))
    tau0 = rmin(scan(inf, lambda c, x, _: jnp.minimum(c, jnp.where(x >= lo, x, jnp.inf))))

    def refine_pass(tau):
        def fn(c, x, _):
            gt = x > tau
            return (c[0] + jnp.where(gt, 1.0, 0.0), jnp.minimum(c[1], jnp.where(gt, x, jnp.inf)),
                    c[2] + jnp.where(x == tau, 1.0, 0.0))
        cgt, nxt, ceq = scan((zero, inf, zero), fn)
        return rsum(cgt), rmin(nxt), rsum(ceq)

    def refine_body(state):
        tau, _, _, _ = state
        cgt, nxt, ceq = refine_pass(tau)
        up = cgt >= kf
        return jnp.where(up, nxt, tau), cgt, ceq, _row_any(up)

    tau, cgt, ceq, _ = lax.while_loop(lambda s: s[3], refine_body, (tau0, zero, zero, jnp.bool_(True)))

    need = kf - cgt
    excess = ceq > need
    jl_ref[...] = inf

    @pl.when(_row_any(excess))
    def _():
        def idx_bisect(_, lh):
            lo_j, hi_j = lh
            mid = jnp.floor((lo_j + hi_j) * 0.5)
            cnt = rsum(scan(zero, lambda c, x, col0: c + jnp.where((x == tau) & (col0 + lane <= mid), 1.0, 0.0)))
            ge = cnt >= need
            return jnp.where(ge, lo_j, mid), jnp.where(ge, mid, hi_j)

        ncol = (nch * cw).astype(F32) if hasattr(nch, "astype") else float(nch * cw)
        _, hi_j = lax.fori_loop(0, 14, idx_bisect, (zero - 1.0, zero + (ncol - 1.0)))
        jl_ref[...] = jnp.where(excess, hi_j, jnp.inf)

    return tau


ATT_TQ = 128
ATT_TK = 512
IDX_TN = 256

_NT = (((1,), (1,)), ((), ()))


def _kv_lane_offset(head):
    return ((head // 2) % 2) * HEAD_DIM


def _assemble_heads(per_head):
    lane = lax.broadcasted_iota(jnp.int32, per_head[0].shape, 1)
    cols = []
    for c in range(N_HEADS // 2):
        a, b = per_head[2 * c], per_head[2 * c + 1]
        if _kv_lane_offset(2 * c) == 0:
            b = pltpu.roll(b, HEAD_DIM, 1)
        else:
            a = pltpu.roll(a, HEAD_DIM, 1)
        cols.append(jnp.where(lane < HEAD_DIM, a, b))
    return jnp.concatenate(cols, axis=1)


def _attn_prompt_kernel(q_ref, qi_ref, kiw_ref, kib_ref, kb_ref, vb_ref, o_ref,
                        sc_ref, wb_ref, jl_ref, tau_ref, bias_ref, m_ref, al_ref, pm_ref, acc_ref, s_ref, p_ref,
                        *, n_keep):
    tq, tk = ATT_TQ, ATT_TK
    qb = pl.program_id(1)
    t0 = qb * tq
    nkb = (t0 + tq + tk - 1) // tk

    for h in range(IDX_HEADS):
        wb_ref[h] = _lane_bcast(kiw_ref[:, IDX_DIM + h:IDX_DIM + h + 1])

    row_g = t0 + lax.broadcasted_iota(jnp.int32, (tq, IDX_TN), 0)
    col_l = lax.broadcasted_iota(jnp.int32, (tq, IDX_TN), 1)

    def score_body(c, _):
        for j in range(tk // IDX_TN):
            c0 = pl.multiple_of(c * tk + j * IDX_TN, IDX_TN)
            kt = kib_ref[0, pl.ds(c0, IDX_TN), :]
            acc = jnp.zeros((tq, IDX_TN), F32)
            for h in range(IDX_HEADS):
                y = lax.dot_general(qi_ref[:, h * LANES:(h + 1) * LANES], kt, _NT, preferred_element_type=F32)
                wb = wb_ref[h]
                acc = acc + jnp.maximum(y, 0.0) * jnp.concatenate([wb] * (IDX_TN // LANES), axis=1)
            sc_ref[:, pl.ds(c0, IDX_TN)] = jnp.where(c0 + col_l <= row_g, acc * IDX_SCALE, -jnp.inf)
        return 0

    lax.fori_loop(0, nkb, score_body, 0)

    tau_ref[...] = _select_rows(sc_ref, nkb, tk, n_keep, jl_ref)

    m_ref[...] = jnp.full(m_ref.shape, NEG_BIG, F32)
    acc_ref[...] = jnp.zeros(acc_ref.shape, F32)
    col_k = lax.broadcasted_iota(jnp.int32, (tq, tk), 1).astype(F32)
    heads_per_pair = N_HEADS // (N_KV_HEADS // 2)
    ncg = tk // LANES

    def tile_lanes(a):
        return jnp.concatenate([a] * ncg, axis=1)

    def attn_body(c, _):
        c0 = pl.multiple_of(c * tk, tk)
        x = sc_ref[:, pl.ds(c0, tk)]
        tau = tile_lanes(tau_ref[...])
        sel = (x > tau) | ((x == tau) & (col_k + c0.astype(F32) <= tile_lanes(jl_ref[...])))
        bias_ref[...] = jnp.where(sel, 0.0, NEG_BIG)

        for h in range(N_HEADS):
            pair = h // heads_per_pair
            kt = kb_ref[0, pl.ds(c0, tk), pair * LANES:(pair + 1) * LANES]
            s = lax.dot_general(q_ref[:, h * LANES:(h + 1) * LANES], kt, _NT, preferred_element_type=F32)
            s = s + bias_ref[...]
            s_ref[h] = s
            pm = s[:, 0:LANES]
            for j in range(1, ncg):
                pm = jnp.maximum(pm, s[:, j * LANES:(j + 1) * LANES])
            pm_ref[h] = pm

        for h in range(N_HEADS):
            m_old = m_ref[h]
            m_new = jnp.maximum(m_old, _lane_bcast(jnp.max(pm_ref[h], axis=-1, keepdims=True)))
            al_ref[h] = jnp.exp(m_old - m_new)
            m_ref[h] = m_new

        for h in range(N_HEADS):
            p_ref[h] = jnp.exp(s_ref[h] - tile_lanes(m_ref[h])).astype(BF16)

        for h in range(N_HEADS):
            pair = h // heads_per_pair
            vt = vb_ref[0, pl.ds(c0, tk), pair * 2 * LANES:(pair + 1) * 2 * LANES]
            alpha = al_ref[h]
            acc_ref[h] = (jnp.concatenate([alpha, alpha], axis=1) * acc_ref[h]
                          + jnp.dot(p_ref[h], vt, preferred_element_type=F32))
        return 0

    lax.fori_loop(0, nkb, attn_body, 0)
    o_ref[...] = _assemble_heads([acc_ref[h, :, :LANES] / acc_ref[h, :, LANES:]
                                  for h in range(N_HEADS)]).astype(BF16)


def _attn_prompt(q, qi, kiw, kib, kb, vb, B, S):
    tq = ATT_TQ
    nq = S // tq
    n_keep = min(TOPK_MAX, S // 4)
    row = lambda b, j: (b * nq + j, 0)
    per_b = lambda b, j: (b, 0, 0)
    return pl.pallas_call(
        functools.partial(_attn_prompt_kernel, n_keep=n_keep),
        grid=(B, nq),
        in_specs=[pl.BlockSpec((tq, N_HEADS * LANES), row), pl.BlockSpec((tq, IDX_HEADS * LANES), row),
                  pl.BlockSpec((tq, LANES), row),
                  pl.BlockSpec((1, S, LANES), per_b), pl.BlockSpec((1, S, KV_W), per_b),
                  pl.BlockSpec((1, S, 2 * KV_W), per_b)],
        out_specs=pl.BlockSpec((tq, N_HEADS * HEAD_DIM), row),
        out_shape=jax.ShapeDtypeStruct((B * S, N_HEADS * HEAD_DIM), BF16),
        scratch_shapes=[pltpu.VMEM((tq, S), F32), pltpu.VMEM((IDX_HEADS, tq, LANES), F32),
                        pltpu.VMEM((tq, LANES), F32), pltpu.VMEM((tq, LANES), F32)]
                       + [pltpu.VMEM((tq, ATT_TK), F32)]
                       + [pltpu.VMEM((N_HEADS, tq, LANES), F32)] * 3
                       + [pltpu.VMEM((N_HEADS, tq, 2 * LANES), F32)]
                       + [pltpu.VMEM((N_HEADS, tq, ATT_TK), F32), pltpu.VMEM((N_HEADS, tq, ATT_TK), BF16)],
        compiler_params=_cparams(("arbitrary", "arbitrary")),
        name="attn_prompt",
    )(q, qi, kiw, kib.reshape(B, S, LANES), kb.reshape(B, S, KV_W), vb.reshape(B, S, 2 * KV_W))


SCORE_PAGES = 16
ATTN_PAGES = 16
SEL_ROWS = 128
SEL_CW = 640


def _sample_scores_kernel(pt_ref, qi_ref, w_ref, kin_ref, *rest):
    del pt_ref
    page_refs, o_ref = rest[:SCORE_PAGES], rest[SCORE_PAGES]
    g = pl.program_id(1)
    t = o_ref.shape[1]
    qi = qi_ref[0]
    wb = _lane_bcast(w_ref[0])

    def page_scores(kpage):
        y = lax.dot_general(qi, kpage.astype(BF16), _NT, preferred_element_type=F32)
        z = jnp.maximum(y, 0.0) * wb
        tot = z[0:t]
        for h in range(1, IDX_HEADS):
            tot = tot + z[h * t:(h + 1) * t]
        return tot * IDX_SCALE

    for j in range(SCORE_PAGES):
        col0 = pl.multiple_of((g * SCORE_PAGES + j) * PAGE_SIZE, PAGE_SIZE)
        o_ref[0, :, pl.ds(col0, PAGE_SIZE)] = page_scores(page_refs[j][0])

    @pl.when(g == 0)
    def _():
        past = o_ref.shape[2] - PAGE_SIZE
        qpos = lax.broadcasted_iota(jnp.int32, (t, PAGE_SIZE), 0)
        kpos = lax.broadcasted_iota(jnp.int32, (t, PAGE_SIZE), 1)
        o_ref[0, :, past:] = jnp.where(kpos <= qpos, page_scores(kin_ref[0]), -jnp.inf)


def _sample_scores(page_table, qi_s, w_s, ki_new_pad, cache_kidx_l):
    B, n_pages = page_table.shape
    t = qi_s.shape[1] // IDX_HEADS
    past = n_pages * PAGE_SIZE
    per_b = lambda b, g, pt: (b, 0, 0)
    page_specs = [pl.BlockSpec((1, PAGE_SIZE, IDX_DIM),
                               functools.partial(lambda b, g, pt, j: (pt[b, g * SCORE_PAGES + j], 0, 0), j=j))
                  for j in range(SCORE_PAGES)]
    return pl.pallas_call(
        _sample_scores_kernel,
        grid_spec=pltpu.PrefetchScalarGridSpec(
            num_scalar_prefetch=1,
            grid=(B, n_pages // SCORE_PAGES),
            in_specs=[pl.BlockSpec((1, IDX_HEADS * t, IDX_DIM), per_b), pl.BlockSpec((1, IDX_HEADS * t, 1), per_b),
                      pl.BlockSpec((1, PAGE_SIZE, IDX_DIM), per_b)] + page_specs,
            out_specs=pl.BlockSpec((1, t, past + PAGE_SIZE), per_b)),
        out_shape=jax.ShapeDtypeStruct((B, t, past + PAGE_SIZE), F32),
        compiler_params=_cparams(("arbitrary", "arbitrary")),
        name="sample_scores",
    )(page_table, qi_s, w_s, ki_new_pad, *([cache_kidx_l] * SCORE_PAGES))


def _select_kernel(sc_ref, tau_ref, jl_ref, *, n_keep):
    tau_ref[...] = _select_rows(sc_ref, sc_ref.shape[1] // SEL_CW, SEL_CW, n_keep, jl_ref)


def _sample_select(scores2d, n_keep):
    r, l = scores2d.shape
    rows = min(SEL_ROWS, r)
    row = lambda i: (i, 0)
    return pl.pallas_call(
        functools.partial(_select_kernel, n_keep=n_keep),
        grid=(r // rows,),
        in_specs=[pl.BlockSpec((rows, l), row)],
        out_specs=[pl.BlockSpec((rows, LANES), row), pl.BlockSpec((rows, LANES), row)],
        out_shape=[jax.ShapeDtypeStruct((r, LANES), F32), jax.ShapeDtypeStruct((r, LANES), F32)],
        compiler_params=_cparams(("arbitrary",)),
        name="sample_select",
    )(scores2d)


def _sample_attn_kernel(pt_ref, q_ref, sc_ref, tau_ref, jl_ref, kn_ref, vn_ref, *rest):
    del pt_ref
    k_refs, v_refs = rest[:ATTN_PAGES], rest[ATTN_PAGES:2 * ATTN_PAGES]
    o_ref, m_ref, l_ref, acc_ref, kbuf_ref, vbuf_ref = rest[2 * ATTN_PAGES:]
    g = pl.program_id(1)
    t = sc_ref.shape[1]
    half = N_HEADS // 2 * t
    span = ATTN_PAGES * PAGE_SIZE

    @pl.when(g == 0)
    def _():
        m_ref[...] = jnp.full(m_ref.shape, NEG_BIG, F32)
        l_ref[...] = jnp.zeros(l_ref.shape, F32)
        acc_ref[...] = jnp.zeros(acc_ref.shape, F32)

    tau = tau_ref[0][:, 0:1]
    jl = jl_ref[0][:, 0:1]

    def attend(k_of_pair, v_of_pair, x, col0f):
        col = lax.broadcasted_iota(jnp.int32, x.shape, 1).astype(F32) + col0f
        sel = (x > tau) | ((x == tau) & (col <= jl))
        sel = jnp.concatenate([jnp.where(sel, 1.0, 0.0)] * (N_HEADS // 2), axis=0) > 0.5
        for pair in range(N_KV_HEADS // 2):
            rows = slice(pair * half, (pair + 1) * half)
            s = lax.dot_general(q_ref[0, rows, :], k_of_pair(pair), _NT, preferred_element_type=F32)
            s = jnp.where(sel, s, NEG_BIG)
            m_old = m_ref[rows]
            m_new = jnp.maximum(m_old, _lane_bcast(jnp.max(s, axis=-1, keepdims=True)))
            alpha = jnp.exp(m_old - m_new)
            p = jnp.exp(s - m_new[:, 0:1])
            l_ref[rows] = alpha * l_ref[rows] + _lane_bcast(jnp.sum(p, axis=-1, keepdims=True))
            acc_ref[rows] = alpha * acc_ref[rows] + jnp.dot(p.astype(BF16), v_of_pair(pair),
                                                            preferred_element_type=F32)
            m_ref[rows] = m_new

    for j in range(ATTN_PAGES):
        kbuf_ref[j * PAGE_SIZE:(j + 1) * PAGE_SIZE, :] = k_refs[j][0].astype(BF16)
        vbuf_ref[j * PAGE_SIZE:(j + 1) * PAGE_SIZE, :] = v_refs[j][0].astype(BF16)
    col0 = pl.multiple_of(g * span, span)
    attend(lambda pair: kbuf_ref[:, pair * LANES:(pair + 1) * LANES],
           lambda pair: vbuf_ref[:, pair * LANES:(pair + 1) * LANES],
           sc_ref[0, :, pl.ds(col0, span)], col0.astype(F32))

    @pl.when(g == pl.num_programs(1) - 1)
    def _():
        past = sc_ref.shape[2] - PAGE_SIZE
        attend(lambda pair: kn_ref[0, :, pair * LANES:(pair + 1) * LANES],
               lambda pair: vn_ref[0, :, pair * LANES:(pair + 1) * LANES],
               sc_ref[0, :, past:], float(past))
        o_ref[0] = acc_ref[...] / l_ref[...]


def _sample_attn(page_table, q_s, scores, tau, jl, k_new_pad, v_new_pad, cache_k_l, cache_v_l):
    B, n_pages = page_table.shape
    t = scores.shape[1]
    rows = N_HEADS * t
    per_b = lambda b, g, pt: (b, 0, 0)

    def page_specs():
        return [pl.BlockSpec((1, PAGE_SIZE, KV_W),
                             functools.partial(lambda b, g, pt, j: (pt[b, g * ATTN_PAGES + j], 0, 0), j=j))
                for j in range(ATTN_PAGES)]

    return pl.pallas_call(
        _sample_attn_kernel,
        grid_spec=pltpu.PrefetchScalarGridSpec(
            num_scalar_prefetch=1,
            grid=(B, n_pages // ATTN_PAGES),
            in_specs=[pl.BlockSpec((1, rows, LANES), per_b), pl.BlockSpec((1, t, scores.shape[2]), per_b),
                      pl.BlockSpec((1, t, LANES), per_b), pl.BlockSpec((1, t, LANES), per_b),
                      pl.BlockSpec((1, PAGE_SIZE, KV_W), per_b), pl.BlockSpec((1, PAGE_SIZE, KV_W), per_b)]
                     + page_specs() + page_specs(),
            out_specs=pl.BlockSpec((1, rows, LANES), per_b),
            scratch_shapes=[pltpu.VMEM((rows, LANES), F32)] * 3
                           + [pltpu.VMEM((ATTN_PAGES * PAGE_SIZE, KV_W), BF16)] * 2),
        out_shape=jax.ShapeDtypeStruct((B, rows, LANES), F32),
        compiler_params=_cparams(("arbitrary", "arbitrary")),
        name="sample_attn",
    )(page_table, q_s, scores, tau, jl, k_new_pad, v_new_pad,
      *([cache_k_l] * ATTN_PAGES), *([cache_v_l] * ATTN_PAGES))


def _sample_attention(sr, B, T, page_table, cache_k_l, cache_v_l, cache_kidx_l):
    n_pages = page_table.shape[1]
    n_keep = min(TOPK_MAX, (n_pages * PAGE_SIZE + T) // 4)
    n_pool = cache_k_l.shape[0]

    def head_major(a, width):
        a = a.reshape(B, T, N_HEADS, LANES)[..., :width]
        return jnp.transpose(a, (0, 2, 1, 3)).reshape(B, N_HEADS * T, width)

    def pad_page(a):
        return jnp.pad(a.reshape(B, T, -1), ((0, 0), (0, PAGE_SIZE - T), (0, 0)))

    qi_s = head_major(sr["qi"], IDX_DIM)
    w_s = jnp.transpose(sr["kiw"][:, IDX_DIM:IDX_DIM + IDX_HEADS].reshape(B, T, IDX_HEADS), (0, 2, 1))
    w_s = w_s.reshape(B, IDX_HEADS * T, 1)
    scores = _sample_scores(page_table, qi_s, w_s, pad_page(sr["kiw"][:, :IDX_DIM]), cache_kidx_l)
    tau, jl = _sample_select(scores.reshape(B * T, -1), n_keep)
    o = _sample_attn(page_table, head_major(sr["q"], LANES), scores, tau.reshape(B, T, LANES),
                     jl.reshape(B, T, LANES), pad_page(sr["kb"]), pad_page(sr["v"].astype(BF16)),
                     cache_k_l.reshape(n_pool, PAGE_SIZE, KV_W), cache_v_l.reshape(n_pool, PAGE_SIZE, KV_W))
    o = o.reshape(B, N_HEADS, T, LANES)
    heads = [o[:, h, :, _kv_lane_offset(h):_kv_lane_offset(h) + HEAD_DIM] for h in range(N_HEADS)]
    return jnp.stack(heads, axis=2).reshape(B * T, N_HEADS * HEAD_DIM).astype(BF16)


def _layer(x, mods, conv_past, attn_fn, p, tiles):
    B, T, _ = x.shape
    sh1, sc1, g1, sh2, sc2, g2 = mods
    pr = _inproj(x, sc1, sh1, p["norm1_g"], p["w_pack"], p["q_norm_g"], p["k_norm_g"], *tiles["inproj"])
    cin = pr["cin"].reshape(B, T, CONV_CH)
    if conv_past is None:
        yc = _conv_prompt(pr["cin"], B, T, p["conv_w"], p["conv_b"], p["ln_g"], p["ln_b"])
        conv_state = cin[:, T - (CONV_W - 1):]
    else:
        cpad = jnp.concatenate([conv_past, cin], axis=1)
        yc = _conv_sample(cpad, p["conv_w"], p["conv_b"], p["ln_g"], p["ln_b"]).reshape(B * T, CONV_CH)
        conv_state = cpad[:, T:]
    attn = attn_fn(pr)
    x1, h2, gate = _merge(x, g1, sc2, sh2, yc, attn, pr["gates"], p["w_conv_out"], p["w_attn_o"], p["w_out"],
                          p["norm2_g"], p["w_grp"], p["b_grp"], p["w_exp"], p["b_exp"], *tiles["merge"])
    y = _moe(h2, gate, x1, g2, p["w13"], p["w2b"], *tiles["moe"])
    k = pr["k"].reshape(B, T, N_KV_HEADS, HEAD_DIM)
    v = pr["v"].reshape(B, T, N_KV_HEADS, HEAD_DIM)
    ki = pr["kiw"][:, :IDX_DIM].reshape(B, T, IDX_DIM)
    return y, k, v, ki, conv_state


def kernel(x_prompt, x_sample, cache_k, cache_v, cache_kidx, state_conv, page_table, c_prompt, c_sample,
           w_ada, b_ada, norm1_g, w_in, q_norm_g, k_norm_g, conv_w, conv_b, ln_g, ln_b, w_conv_out, w_attn_o,
           w_out, norm2_g, w_grp, b_grp, w_exp, b_exp, w1, w3, w2):
    depth = w_ada.shape[0]
    Bp, S, _ = x_prompt.shape
    Bs, T, _ = x_sample.shape
    xp, xs = x_prompt, x_sample
    outs = [[] for _ in range(8)]
    c_all = jnp.concatenate([c_prompt, c_sample], axis=0)
    c_all = jnp.pad(c_all, ((0, (-c_all.shape[0]) % 16), (0, 0)))
    tiles_p = {"inproj": (1, min(256, S)), "merge": (1, min(512, S)), "moe": (1, min(1024, S))}
    tiles_s = {"inproj": (min(32, Bs), T), "merge": (min(64, Bs), T), "moe": (min(128, Bs), T)}
    for l in range(depth):
        p = dict(norm1_g=norm1_g[l], w_pack=_pack_w_in(w_in[l]), q_norm_g=q_norm_g[l], k_norm_g=k_norm_g[l],
                 conv_w=conv_w[l], conv_b=conv_b[l], ln_g=ln_g[l], ln_b=ln_b[l], w_conv_out=w_conv_out[l],
                 w_attn_o=w_attn_o[l], w_out=w_out[l], norm2_g=norm2_g[l], w_grp=w_grp[l], b_grp=b_grp[l],
                 w_exp=w_exp[l], b_exp=b_exp[l],
                 w13=jnp.concatenate([w1[l], w3[l]], axis=-1).astype(BF16), w2b=w2[l].astype(BF16))
        mod = _ada(c_all, w_ada[l], b_ada[l])
        mods = [m[:, None, :] for m in jnp.split(mod, 6, axis=-1)]
        mods_p = [m[:Bp] for m in mods]
        mods_s = [m[Bp:Bp + Bs] for m in mods]

        attn_p = lambda pr: _attn_prompt(pr["q"], pr["qi"], pr["kiw"], pr["kib"], pr["kb"], pr["vb"], Bp, S)
        xp, k, v, ki, cv = _layer(xp, mods_p, None, attn_p, p, tiles_p)
        for o, a in zip(outs[:4], (k, v, ki, cv)):
            o.append(a)

        attn_s = lambda pr: _sample_attention(pr, Bs, T, page_table, cache_k[l], cache_v[l], cache_kidx[l])
        xs, k, v, ki, cv = _layer(xs, mods_s, state_conv[l], attn_s, p, tiles_s)
        for o, a in zip(outs[4:], (k, v, ki, cv)):
            o.append(a)
    return (xp, xs) + tuple(jnp.stack(o) for o in outs)
```

```python
import functools

import jax
import jax.numpy as jnp
from jax import lax
from jax.experimental import pallas as pl
from jax.experimental.pallas import tpu as pltpu

F32 = jnp.float32
BF16 = jnp.bfloat16

D_MODEL = 1024
CONV_CH = 512
CONV_W = 31
N_HEADS = 8
HEAD_DIM = 64
N_KV_HEADS = 4
KV_W = N_KV_HEADS * HEAD_DIM
IDX_HEADS = 8
IDX_DIM = 64
IDX_SCALE = (IDX_HEADS * IDX_DIM) ** -0.5
TOPK_MAX = 256
PAGE_SIZE = 128
N_GROUPS = 4
EXP_PER_GROUP = 8
N_EXPERTS = 32
EXPERT_FF = 256
EPS = 1e-6

LANES = 128
NEG_BIG = -1e30
VMEM_LIMIT = 56 * 1024 * 1024

C_GLU = 0
C_Q = C_GLU + 2 * CONV_CH
C_K = C_Q + N_HEADS * LANES
C_V = C_K + KV_W
C_QI = C_V + KV_W
C_KIW = C_QI + IDX_HEADS * LANES
C_GATES = C_KIW + LANES
IN_PACK = C_GATES + 2 * D_MODEL

N_BISECT = 16


def _cparams(sem):
    return pltpu.CompilerParams(dimension_semantics=sem, vmem_limit_bytes=VMEM_LIMIT)


def _sigmoid(x):
    return 1.0 / (1.0 + jnp.exp(-x))


def _silu(x):
    return x * _sigmoid(x)


def _const_spec(shape):
    nd = len(shape)
    return pl.BlockSpec(shape, lambda *_: (0,) * nd)


def _ada_kernel(c_ref, w_ref, b_ref, o_ref):
    s = _silu(c_ref[...])
    o_ref[...] = jnp.dot(s.astype(BF16), w_ref[...], preferred_element_type=F32) + b_ref[...]


def _ada(c, w_ada, b_ada):
    r = c.shape[0]
    tn = 1536
    return pl.pallas_call(
        _ada_kernel,
        grid=(6 * D_MODEL // tn,),
        in_specs=[pl.BlockSpec((r, D_MODEL), lambda j: (0, 0)),
                  pl.BlockSpec((D_MODEL, tn), lambda j: (0, j)),
                  pl.BlockSpec((1, tn), lambda j: (0, j))],
        out_specs=pl.BlockSpec((r, tn), lambda j: (0, j)),
        out_shape=jax.ShapeDtypeStruct((r, 6 * D_MODEL), F32),
        compiler_params=_cparams(("arbitrary",)),
        name="adaln_mod",
    )(c, w_ada.astype(BF16), b_ada.reshape(1, -1))


def _inproj_kernel(x_ref, sc_ref, sh_ref, g_ref, w_ref, qg_ref, kg_ref, ones_ref,
                   cin_ref, q_ref, k_ref, v_ref, kb_ref, vb_ref, qi_ref, kiw_ref, kib_ref, gates_ref):
    bb, tt, _ = x_ref.shape
    x = x_ref[...]
    ms = jnp.mean(x * x, axis=-1, keepdims=True)
    h = x * lax.rsqrt(ms + EPS) * g_ref[...] * (1.0 + sc_ref[...]) + sh_ref[...]
    hb = h.reshape(bb * tt, D_MODEL).astype(BF16)

    def proj(c0, width):
        return jnp.dot(hb, w_ref[:, c0:c0 + width], preferred_element_type=F32)

    glu = proj(C_GLU, 2 * CONV_CH)
    cin_ref[...] = glu[:, :CONV_CH] * _sigmoid(glu[:, CONV_CH:])

    for hd in range(N_HEADS):
        qh = proj(C_Q + hd * LANES, LANES)
        msq = jnp.sum(qh * qh, axis=-1, keepdims=True) * (1.0 / HEAD_DIM)
        qn = qh * lax.rsqrt(msq + EPS) * qg_ref[:, hd * LANES:(hd + 1) * LANES]
        q_ref[:, hd * LANES:(hd + 1) * LANES] = (qn * (HEAD_DIM ** -0.5)).astype(BF16)

    k = proj(C_K, KV_W)
    k2 = k * k
    k2_hi = k2.astype(BF16)
    k2_lo = (k2 - k2_hi.astype(F32)).astype(BF16)
    ksum = (jnp.dot(k2_hi, ones_ref[...], preferred_element_type=F32)
            + jnp.dot(k2_lo, ones_ref[...], preferred_element_type=F32))
    kn = k * lax.rsqrt(ksum * (1.0 / HEAD_DIM) + EPS) * kg_ref[...]
    k_ref[...] = kn
    kb_ref[...] = kn.astype(BF16)

    v = proj(C_V, KV_W)
    v_ref[...] = v
    ones_blk = jnp.ones((v.shape[0], LANES), BF16)
    for pair in range(N_KV_HEADS // 2):
        vb_ref[:, pair * 2 * LANES:pair * 2 * LANES + LANES] = v[:, pair * LANES:(pair + 1) * LANES].astype(BF16)
        vb_ref[:, pair * 2 * LANES + LANES:(pair + 1) * 2 * LANES] = ones_blk

    qi_ref[...] = proj(C_QI, IDX_HEADS * LANES).astype(BF16)

    kiw = proj(C_KIW, LANES)
    kiw_ref[...] = kiw
    lane = lax.broadcasted_iota(jnp.int32, kiw.shape, 1)
    kib_ref[...] = jnp.where(lane < IDX_DIM, kiw, 0.0).astype(BF16)

    gates_ref[...] = _sigmoid(proj(C_GATES, 2 * D_MODEL))


def _pack_w_in(w_in):
    o = 0
    segs = {}
    for name, width in (("glu", 2 * CONV_CH), ("q", N_HEADS * HEAD_DIM), ("k", KV_W), ("v", KV_W),
                        ("qi", IDX_HEADS * IDX_DIM), ("ki", IDX_DIM), ("wi", IDX_HEADS), ("gates", 2 * D_MODEL)):
        segs[name] = w_in[:, o:o + width]
        o += width
    zeros64 = jnp.zeros((D_MODEL, HEAD_DIM), F32)
    cols = [segs["glu"]]
    for hd in range(N_HEADS):
        wq = segs["q"][:, hd * HEAD_DIM:(hd + 1) * HEAD_DIM]
        cols += [zeros64, wq] if (hd // 2) % 2 else [wq, zeros64]
    cols += [segs["k"], segs["v"]]
    for hd in range(IDX_HEADS):
        cols += [segs["qi"][:, hd * IDX_DIM:(hd + 1) * IDX_DIM], zeros64]
    cols += [segs["ki"], segs["wi"], jnp.zeros((D_MODEL, LANES - IDX_DIM - IDX_HEADS), F32)]
    cols += [segs["gates"]]
    return jnp.concatenate(cols, axis=1).astype(BF16)


def _pack_q_gain(q_norm_g):
    z = jnp.zeros((HEAD_DIM,), F32)
    parts = []
    for hd in range(N_HEADS):
        parts += [z, q_norm_g] if (hd // 2) % 2 else [q_norm_g, z]
    return jnp.concatenate(parts).reshape(1, N_HEADS * LANES)


def _inproj(x, sc1, sh1, norm1_g, w_pack, q_norm_g, k_norm_g, bb, tt):
    B, T, _ = x.shape
    n = B * T
    tm = bb * tt
    nb_t = T // tt
    grid = (B // bb, nb_t)
    row = lambda i, j: (i * nb_t + j, 0)
    ones = (lax.broadcasted_iota(jnp.int32, (KV_W, KV_W), 0) // HEAD_DIM
            == lax.broadcasted_iota(jnp.int32, (KV_W, KV_W), 1) // HEAD_DIM).astype(BF16)
    outs = [("cin", CONV_CH, F32), ("q", N_HEADS * LANES, BF16), ("k", KV_W, F32), ("v", KV_W, F32),
            ("kb", KV_W, BF16), ("vb", 2 * KV_W, BF16), ("qi", IDX_HEADS * LANES, BF16), ("kiw", LANES, F32),
            ("kib", LANES, BF16), ("gates", 2 * D_MODEL, F32)]
    res = pl.pallas_call(
        _inproj_kernel,
        grid=grid,
        in_specs=[pl.BlockSpec((bb, tt, D_MODEL), lambda i, j: (i, j, 0)),
                  pl.BlockSpec((bb, 1, D_MODEL), lambda i, j: (i, 0, 0)),
                  pl.BlockSpec((bb, 1, D_MODEL), lambda i, j: (i, 0, 0)),
                  _const_spec((1, 1, D_MODEL)),
                  _const_spec((D_MODEL, IN_PACK)),
                  _const_spec((1, N_HEADS * LANES)),
                  _const_spec((1, KV_W)),
                  _const_spec((KV_W, KV_W))],
        out_specs=[pl.BlockSpec((tm, w), row) for _, w, _ in outs],
        out_shape=[jax.ShapeDtypeStruct((n, w), dt) for _, w, dt in outs],
        compiler_params=_cparams(("arbitrary", "arbitrary")),
        name="in_proj",
    )(x, sc1, sh1, norm1_g.reshape(1, 1, D_MODEL), w_pack, _pack_q_gain(q_norm_g),
      jnp.tile(k_norm_g, N_KV_HEADS).reshape(1, KV_W), ones)
    return {name: r for (name, _, _), r in zip(outs, res)}


CONV_HALO = 32
CONV_ROWS = 32


def _ln_silu(y, g, b):
    mu = jnp.mean(y, axis=-1, keepdims=True)
    yc = y - mu
    var = jnp.mean(yc * yc, axis=-1, keepdims=True)
    return _silu(yc * lax.rsqrt(var + EPS) * g + b)


def _conv_prompt_kernel(cin_ref, w_ref, b_ref, g_ref, lb_ref, o_ref, win_ref):
    tc = cin_ref.shape[0]

    @pl.when(pl.program_id(1) == 0)
    def _():
        win_ref[0:CONV_HALO, :] = jnp.zeros((CONV_HALO, CONV_CH), F32)

    win_ref[CONV_HALO:, :] = cin_ref[...]
    off = CONV_HALO - (CONV_W - 1)
    for r0 in range(0, tc, CONV_ROWS):
        acc = jnp.zeros((CONV_ROWS, CONV_CH), F32) + b_ref[...]
        for j in range(CONV_W):
            acc = acc + win_ref[r0 + off + j:r0 + off + j + CONV_ROWS, :] * w_ref[j:j + 1, :]
        o_ref[r0:r0 + CONV_ROWS, :] = _ln_silu(acc, g_ref[...], lb_ref[...]).astype(BF16)
    win_ref[0:CONV_HALO, :] = cin_ref[tc - CONV_HALO:, :]


def _conv_prompt(cin, B, S, conv_w, conv_b, ln_g, ln_b):
    tc = 256
    nt = S // tc
    return pl.pallas_call(
        _conv_prompt_kernel,
        grid=(B, nt),
        in_specs=[pl.BlockSpec((tc, CONV_CH), lambda b, j: (b * nt + j, 0)),
                  _const_spec((CONV_W, CONV_CH)), _const_spec((1, CONV_CH)),
                  _const_spec((1, CONV_CH)), _const_spec((1, CONV_CH))],
        out_specs=pl.BlockSpec((tc, CONV_CH), lambda b, j: (b * nt + j, 0)),
        out_shape=jax.ShapeDtypeStruct((B * S, CONV_CH), BF16),
        scratch_shapes=[pltpu.VMEM((CONV_HALO + tc, CONV_CH), F32)],
        compiler_params=_cparams(("arbitrary", "arbitrary")),
        name="conv_prompt",
    )(cin, conv_w, conv_b.reshape(1, -1), ln_g.reshape(1, -1), ln_b.reshape(1, -1))


def _conv_sample_kernel(cpad_ref, w_ref, b_ref, g_ref, lb_ref, o_ref):
    bb, tp, _ = cpad_ref.shape
    t = tp - (CONV_W - 1)
    for b in range(bb):
        acc = jnp.zeros((t, CONV_CH), F32) + b_ref[...]
        for j in range(CONV_W):
            acc = acc + cpad_ref[b, j:j + t, :] * w_ref[j:j + 1, :]
        o_ref[b] = _ln_silu(acc, g_ref[...], lb_ref[...]).astype(BF16)


def _conv_sample(cpad, conv_w, conv_b, ln_g, ln_b):
    B, tp, _ = cpad.shape
    t = tp - (CONV_W - 1)
    bb = 8
    return pl.pallas_call(
        _conv_sample_kernel,
        grid=(B // bb,),
        in_specs=[pl.BlockSpec((bb, tp, CONV_CH), lambda i: (i, 0, 0)),
                  _const_spec((CONV_W, CONV_CH)), _const_spec((1, CONV_CH)),
                  _const_spec((1, CONV_CH)), _const_spec((1, CONV_CH))],
        out_specs=pl.BlockSpec((bb, t, CONV_CH), lambda i: (i, 0, 0)),
        out_shape=jax.ShapeDtypeStruct((B, t, CONV_CH), BF16),
        compiler_params=_cparams(("arbitrary",)),
        name="conv_sample",
    )(cpad, conv_w, conv_b.reshape(1, -1), ln_g.reshape(1, -1), ln_b.reshape(1, -1))


R_EXP = 0
R_GRP = N_EXPERTS


def _lane_min_index(mask, lane):
    return jnp.min(jnp.where(mask, lane, float(LANES)), axis=-1, keepdims=True)


def _router_gate(logits):
    lane = lax.broadcasted_iota(jnp.int32, logits.shape, 1).astype(F32)
    is_grp = (lane >= R_GRP) & (lane < R_GRP + N_GROUPS)
    gl = jnp.where(is_grp, logits, -jnp.inf)
    gmax = jnp.max(gl, axis=-1, keepdims=True)
    gexp = jnp.where(is_grp, jnp.exp(gl - gmax), 0.0)
    p_grp = gexp / jnp.sum(gexp, axis=-1, keepdims=True)
    g_top = jnp.max(p_grp, axis=-1, keepdims=True)
    g_idx = _lane_min_index(is_grp & (p_grp == g_top), lane) - R_GRP
    in_grp = (lane >= g_idx * EXP_PER_GROUP) & (lane < (g_idx + 1) * EXP_PER_GROUP)
    el = jnp.where(in_grp, logits, -jnp.inf)
    emax = jnp.max(el, axis=-1, keepdims=True)
    eexp = jnp.where(in_grp, jnp.exp(el - emax), 0.0)
    p_exp = eexp / jnp.sum(eexp, axis=-1, keepdims=True)
    p1 = jnp.max(jnp.where(in_grp, p_exp, -1.0), axis=-1, keepdims=True)
    i1 = _lane_min_index(in_grp & (p_exp == p1), lane)
    rest = in_grp & (lane != i1)
    p2 = jnp.max(jnp.where(rest, p_exp, -1.0), axis=-1, keepdims=True)
    i2 = _lane_min_index(rest & (p_exp == p2), lane)
    tot = p1 + p2
    return jnp.where(lane == i1, g_top * p1 / tot, 0.0) + jnp.where(lane == i2, g_top * p2 / tot, 0.0)


def _merge_kernel(x_ref, g1_ref, sc2_ref, sh2_ref, yc_ref, at_ref, gates_ref, wco_ref, wao_ref, wout_ref,
                  n2g_ref, wr_ref, br_ref, x1_ref, h2_ref, gate_ref):
    bb, tt, _ = x_ref.shape
    y_conv = jnp.dot(yc_ref[...], wco_ref[...], preferred_element_type=F32)
    y_attn = jnp.dot(at_ref[...], wao_ref[...], preferred_element_type=F32)
    m = gates_ref[:, :D_MODEL] * y_conv + gates_ref[:, D_MODEL:] * y_attn
    mo = jnp.dot(m.astype(BF16), wout_ref[...], preferred_element_type=F32)
    x1 = x_ref[...] + g1_ref[...] * mo.reshape(bb, tt, D_MODEL)
    x1_ref[...] = x1
    ms = jnp.mean(x1 * x1, axis=-1, keepdims=True)
    h2 = (x1 * lax.rsqrt(ms + EPS) * n2g_ref[...] * (1.0 + sc2_ref[...]) + sh2_ref[...]).reshape(bb * tt, D_MODEL)
    h2_ref[...] = h2.astype(BF16)
    logits = jnp.dot(h2, wr_ref[...], preferred_element_type=F32, precision=lax.Precision.HIGHEST) + br_ref[...]
    gate_ref[...] = _router_gate(logits)


def _merge(x, g1, sc2, sh2, yc, attn, gates, w_conv_out, w_attn_o, w_out, norm2_g, w_grp, b_grp, w_exp, b_exp,
           bb, tt):
    B, T, _ = x.shape
    n = B * T
    tm = bb * tt
    nb_t = T // tt
    row = lambda i, j: (i * nb_t + j, 0)
    mod = lambda i, j: (i, 0, 0)
    pad = jnp.zeros((D_MODEL, LANES - N_EXPERTS - N_GROUPS), F32)
    wr = jnp.concatenate([w_exp, w_grp, pad], axis=1)
    br = jnp.concatenate([b_exp, b_grp, jnp.zeros((LANES - N_EXPERTS - N_GROUPS,), F32)]).reshape(1, LANES)
    return pl.pallas_call(
        _merge_kernel,
        grid=(B // bb, nb_t),
        in_specs=[pl.BlockSpec((bb, tt, D_MODEL), lambda i, j: (i, j, 0)),
                  pl.BlockSpec((bb, 1, D_MODEL), mod), pl.BlockSpec((bb, 1, D_MODEL), mod),
                  pl.BlockSpec((bb, 1, D_MODEL), mod),
                  pl.BlockSpec((tm, CONV_CH), row), pl.BlockSpec((tm, N_HEADS * HEAD_DIM), row),
                  pl.BlockSpec((tm, 2 * D_MODEL), row),
                  _const_spec((CONV_CH, D_MODEL)), _const_spec((N_HEADS * HEAD_DIM, D_MODEL)),
                  _const_spec((D_MODEL, D_MODEL)), _const_spec((1, 1, D_MODEL)),
                  _const_spec((D_MODEL, LANES)), _const_spec((1, LANES))],
        out_specs=[pl.BlockSpec((bb, tt, D_MODEL), lambda i, j: (i, j, 0)),
                   pl.BlockSpec((tm, D_MODEL), row), pl.BlockSpec((tm, LANES), row)],
        out_shape=[jax.ShapeDtypeStruct((B, T, D_MODEL), F32), jax.ShapeDtypeStruct((n, D_MODEL), BF16),
                   jax.ShapeDtypeStruct((n, LANES), F32)],
        compiler_params=_cparams(("arbitrary", "arbitrary")),
        name="merge_router",
    )(x, g1, sc2, sh2, yc, attn, gates, w_conv_out.astype(BF16), w_attn_o.astype(BF16), w_out.astype(BF16),
      norm2_g.reshape(1, 1, D_MODEL), wr, br)


def _moe_kernel(h2_ref, gate_ref, w13_ref, w2_ref, x1_ref, g2_ref, o_ref, acc_ref):
    e = pl.program_id(2)
    bb, tt, _ = x1_ref.shape

    @pl.when(e == 0)
    def _():
        acc_ref[...] = jnp.zeros_like(acc_ref)

    ab = jnp.dot(h2_ref[...], w13_ref[0], preferred_element_type=F32)
    lane = lax.broadcasted_iota(jnp.int32, gate_ref.shape, 1)
    gcol = jnp.sum(jnp.where(lane == e, gate_ref[...], 0.0), axis=-1, keepdims=True)
    hid = _silu(ab[:, :EXPERT_FF]) * ab[:, EXPERT_FF:] * gcol
    acc_ref[...] += jnp.dot(hid.astype(BF16), w2_ref[0], preferred_element_type=F32)

    @pl.when(e == N_EXPERTS - 1)
    def _():
        o_ref[...] = x1_ref[...] + g2_ref[...] * acc_ref[...].reshape(bb, tt, D_MODEL)


def _moe(h2, gate, x1, g2, w13, w2b, bb, tt):
    B, T, _ = x1.shape
    tm = bb * tt
    nb_t = T // tt
    row = lambda i, j, e: (i * nb_t + j, 0)
    return pl.pallas_call(
        _moe_kernel,
        grid=(B // bb, nb_t, N_EXPERTS),
        in_specs=[pl.BlockSpec((tm, D_MODEL), row), pl.BlockSpec((tm, LANES), row),
                  pl.BlockSpec((1, D_MODEL, 2 * EXPERT_FF), lambda i, j, e: (e, 0, 0)),
                  pl.BlockSpec((1, EXPERT_FF, D_MODEL), lambda i, j, e: (e, 0, 0)),
                  pl.BlockSpec((bb, tt, D_MODEL), lambda i, j, e: (i, j, 0)),
                  pl.BlockSpec((bb, 1, D_MODEL), lambda i, j, e: (i, 0, 0))],
        out_specs=pl.BlockSpec((bb, tt, D_MODEL), lambda i, j, e: (i, j, 0)),
        out_shape=jax.ShapeDtypeStruct((B, T, D_MODEL), F32),
        scratch_shapes=[pltpu.VMEM((tm, D_MODEL), F32)],
        compiler_params=_cparams(("arbitrary", "arbitrary", "arbitrary")),
        name="moe_dense",
    )(h2, gate, w13, w2b, x1, g2)


def _lane_bcast(col):
    return jnp.broadcast_to(col, (col.shape[0], LANES))


def _row_any(flag_rows):
    return jnp.max(jnp.where(flag_rows, 1.0, 0.0)) > 0.0


def _select_rows(sc_ref, nch, cw, n_keep, jl_ref):
    R = sc_ref.shape[0]
    ng = cw // LANES
    kf = float(n_keep)
    inf = jnp.full((R, LANES), jnp.inf, F32)
    zero = jnp.zeros((R, LANES), F32)
    lane = lax.broadcasted_iota(jnp.int32, (R, LANES), 1).astype(F32)

    def scan(init, fn):
        def body(c, carry):
            base = pl.multiple_of(c * cw, LANES)
            for j in range(ng):
                blk = sc_ref[:, pl.ds(base + j * LANES, LANES)]
                carry = fn(carry, blk, (base + j * LANES).astype(F32))
            return carry
        return lax.fori_loop(0, nch, body, init)

    rsum = lambda a: _lane_bcast(jnp.sum(a, axis=-1, keepdims=True))
    rmin = lambda a: _lane_bcast(jnp.min(a, axis=-1, keepdims=True))
    rmax = lambda a: _lane_bcast(jnp.max(a, axis=-1, keepdims=True))

    mx, mn = scan((-inf, inf), lambda c, x, _: (jnp.maximum(c[0], x),
                                                 jnp.minimum(c[1], jnp.where(x == -jnp.inf, jnp.inf, x))))
    hi0, lo0 = rmax(mx), rmin(mn)

    def bisect(_, lh):
        lo, hi = lh
        mid = 0.5 * lo + 0.5 * hi
        cnt = rsum(scan(zero, lambda c, x, _: c + jnp.where(x >= mid, 1.0, 0.0)))
        ge = cnt >= kf
        return jnp.where(ge, mid, lo), jnp.where(ge, hi, mid)

    lo, _ = lax.fori_loop(0, N_BISECT, bisect, (lo0, hi0))
    tau0 = rmin(scan(inf, lambda c, x, _: jnp.minimum(c, jnp.where(x >= lo, x, jnp.inf))))

    def refine_pass(tau):
        def fn(c, x, _):
            gt = x > tau
            return (c[0] + jnp.where(gt, 1.0, 0.0), jnp.minimum(c[1], jnp.where(gt, x, jnp.inf)),
                    c[2] + jnp.where(x == tau, 1.0, 0.0))
        cgt, nxt, ceq = scan((zero, inf, zero), fn)
        return rsum(cgt), rmin(nxt), rsum(ceq)

    def refine_body(state):
        tau, _, _, _ = state
        cgt, nxt, ceq = refine_pass(tau)
        up = cgt >= kf
        return jnp.where(up, nxt, tau), cgt, ceq, _row_any(up)

    tau, cgt, ceq, _ = lax.while_loop(lambda s: s[3], refine_body, (tau0, zero, zero, jnp.bool_(True)))

    need = kf - cgt
    excess = ceq > need
    jl_ref[...] = inf

    @pl.when(_row_any(excess))
    def _():
        def idx_bisect(_, lh):
            lo_j, hi_j = lh
            mid = jnp.floor((lo_j + hi_j) * 0.5)
            cnt = rsum(scan(zero, lambda c, x, col0: c + jnp.where((x == tau) & (col0 + lane <= mid), 1.0, 0.0)))
            ge = cnt >= need
            return jnp.where(ge, lo_j, mid), jnp.where(ge, mid, hi_j)

        ncol = (nch * cw).astype(F32) if hasattr(nch, "astype") else float(nch * cw)
        _, hi_j = lax.fori_loop(0, 14, idx_bisect, (zero - 1.0, zero + (ncol - 1.0)))
        jl_ref[...] = jnp.where(excess, hi_j, jnp.inf)

    return tau


ATT_TQ = 128
ATT_TK = 512
IDX_TN = 256
SEL_CHUNK = 1024

_NT = (((1,), (1,)), ((), ()))


def _kv_lane_offset(head):
    return ((head // 2) % 2) * HEAD_DIM


def _assemble_heads(per_head):
    lane = lax.broadcasted_iota(jnp.int32, per_head[0].shape, 1)
    cols = []
    for c in range(N_HEADS // 2):
        a, b = per_head[2 * c], per_head[2 * c + 1]
        if _kv_lane_offset(2 * c) == 0:
            b = pltpu.roll(b, HEAD_DIM, 1)
        else:
            a = pltpu.roll(a, HEAD_DIM, 1)
        cols.append(jnp.where(lane < HEAD_DIM, a, b))
    return jnp.concatenate(cols, axis=1)


def _attn_prompt_kernel(q_ref, qi_ref, kiw_ref, kib_ref, kb_ref, vb_ref, o_ref,
                        sc_ref, wb_ref, jl_ref, tau_ref, bias_ref, m_ref, al_ref, pm_ref, acc_ref, s_ref, p_ref,
                        *, n_keep):
    tq, tk = ATT_TQ, ATT_TK
    qb = pl.program_id(1)
    t0 = qb * tq
    nkb = (t0 + tq + tk - 1) // tk

    for h in range(IDX_HEADS):
        wb_ref[h] = _lane_bcast(kiw_ref[:, IDX_DIM + h:IDX_DIM + h + 1])

    row_g = t0 + lax.broadcasted_iota(jnp.int32, (tq, IDX_TN), 0)
    col_l = lax.broadcasted_iota(jnp.int32, (tq, IDX_TN), 1)

    def score_body(c, _):
        for j in range(tk // IDX_TN):
            c0 = pl.multiple_of(c * tk + j * IDX_TN, IDX_TN)
            kt = kib_ref[0, pl.ds(c0, IDX_TN), :]
            acc = jnp.zeros((tq, IDX_TN), F32)
            for h in range(IDX_HEADS):
                y = lax.dot_general(qi_ref[:, h * LANES:(h + 1) * LANES], kt, _NT, preferred_element_type=F32)
                wb = wb_ref[h]
                acc = acc + jnp.maximum(y, 0.0) * jnp.concatenate([wb] * (IDX_TN // LANES), axis=1)
            sc_ref[:, pl.ds(c0, IDX_TN)] = jnp.where(c0 + col_l <= row_g, acc * IDX_SCALE, -jnp.inf)
        return 0

    lax.fori_loop(0, nkb, score_body, 0)

    per_chunk = SEL_CHUNK // tk
    for k in range(1, per_chunk):
        @pl.when(nkb % per_chunk == k)
        def _():
            for j in range(k, per_chunk):
                sc_ref[:, pl.ds(pl.multiple_of((nkb - k + j) * tk, tk), tk)] = jnp.full((tq, tk), -jnp.inf, F32)

    tau_ref[...] = _select_rows(sc_ref, (nkb + per_chunk - 1) // per_chunk, SEL_CHUNK, n_keep, jl_ref)

    m_ref[...] = jnp.full(m_ref.shape, NEG_BIG, F32)
    acc_ref[...] = jnp.zeros(acc_ref.shape, F32)
    col_k = lax.broadcasted_iota(jnp.int32, (tq, tk), 1).astype(F32)
    heads_per_pair = N_HEADS // (N_KV_HEADS // 2)
    ncg = tk // LANES

    def tile_lanes(a):
        return jnp.concatenate([a] * ncg, axis=1)

    def attn_body(c, _):
        c0 = pl.multiple_of(c * tk, tk)
        x = sc_ref[:, pl.ds(c0, tk)]
        tau = tile_lanes(tau_ref[...])
        sel = (x > tau) | ((x == tau) & (col_k + c0.astype(F32) <= tile_lanes(jl_ref[...])))
        bias_ref[...] = jnp.where(sel, 0.0, NEG_BIG)

        for h in range(N_HEADS):
            pair = h // heads_per_pair
            kt = kb_ref[0, pl.ds(c0, tk), pair * LANES:(pair + 1) * LANES]
            s = lax.dot_general(q_ref[:, h * LANES:(h + 1) * LANES], kt, _NT, preferred_element_type=F32)
            s = s + bias_ref[...]
            s_ref[h] = s
            pm = s[:, 0:LANES]
            for j in range(1, ncg):
                pm = jnp.maximum(pm, s[:, j * LANES:(j + 1) * LANES])
            pm_ref[h] = pm

        for h in range(N_HEADS):
            m_old = m_ref[h]
            m_new = jnp.maximum(m_old, _lane_bcast(jnp.max(pm_ref[h], axis=-1, keepdims=True)))
            al_ref[h] = jnp.exp(m_old - m_new)
            m_ref[h] = m_new

        for h in range(N_HEADS):
            p_ref[h] = jnp.exp(s_ref[h] - tile_lanes(m_ref[h])).astype(BF16)

        for h in range(N_HEADS):
            pair = h // heads_per_pair
            vt = vb_ref[0, pl.ds(c0, tk), pair * 2 * LANES:(pair + 1) * 2 * LANES]
            alpha = al_ref[h]
            acc_ref[h] = (jnp.concatenate([alpha, alpha], axis=1) * acc_ref[h]
                          + jnp.dot(p_ref[h], vt, preferred_element_type=F32))
        return 0

    lax.fori_loop(0, nkb, attn_body, 0)
    o_ref[...] = _assemble_heads([acc_ref[h, :, :LANES] / acc_ref[h, :, LANES:]
                                  for h in range(N_HEADS)]).astype(BF16)


def _attn_prompt(q, qi, kiw, kib, kb, vb, B, S):
    tq = ATT_TQ
    assert S % SEL_CHUNK == 0 and SEL_CHUNK % ATT_TK == 0 and ATT_TK % tq == 0
    nq = S // tq
    n_keep = min(TOPK_MAX, S // 4)
    row = lambda b, j: (b * nq + j, 0)
    per_b = lambda b, j: (b, 0, 0)
    return pl.pallas_call(
        functools.partial(_attn_prompt_kernel, n_keep=n_keep),
        grid=(B, nq),
        in_specs=[pl.BlockSpec((tq, N_HEADS * LANES), row), pl.BlockSpec((tq, IDX_HEADS * LANES), row),
                  pl.BlockSpec((tq, LANES), row),
                  pl.BlockSpec((1, S, LANES), per_b), pl.BlockSpec((1, S, KV_W), per_b),
                  pl.BlockSpec((1, S, 2 * KV_W), per_b)],
        out_specs=pl.BlockSpec((tq, N_HEADS * HEAD_DIM), row),
        out_shape=jax.ShapeDtypeStruct((B * S, N_HEADS * HEAD_DIM), BF16),
        scratch_shapes=[pltpu.VMEM((tq, S), F32), pltpu.VMEM((IDX_HEADS, tq, LANES), F32),
                        pltpu.VMEM((tq, LANES), F32), pltpu.VMEM((tq, LANES), F32)]
                       + [pltpu.VMEM((tq, ATT_TK), F32)]
                       + [pltpu.VMEM((N_HEADS, tq, LANES), F32)] * 3
                       + [pltpu.VMEM((N_HEADS, tq, 2 * LANES), F32)]
                       + [pltpu.VMEM((N_HEADS, tq, ATT_TK), F32), pltpu.VMEM((N_HEADS, tq, ATT_TK), BF16)],
        compiler_params=_cparams(("arbitrary", "arbitrary")),
        name="attn_prompt",
    )(q, qi, kiw, kib.reshape(B, S, LANES), kb.reshape(B, S, KV_W), vb.reshape(B, S, 2 * KV_W))


SCORE_PAGES = 16
ATTN_PAGES = 16
SEL_ROWS = 128
SEL_CW = 640


def _sample_scores_kernel(pt_ref, qi_ref, w_ref, kin_ref, *rest):
    del pt_ref
    page_refs, o_ref = rest[:SCORE_PAGES], rest[SCORE_PAGES]
    g = pl.program_id(1)
    t = o_ref.shape[1]
    qi = qi_ref[0]
    wb = _lane_bcast(w_ref[0])

    def page_scores(kpage_t):
        y = jnp.dot(qi, kpage_t.astype(BF16), preferred_element_type=F32)
        z = jnp.maximum(y, 0.0) * wb
        tot = z[0:t]
        for h in range(1, IDX_HEADS):
            tot = tot + z[h * t:(h + 1) * t]
        return tot * IDX_SCALE

    for j in range(SCORE_PAGES):
        col0 = pl.multiple_of((g * SCORE_PAGES + j) * PAGE_SIZE, PAGE_SIZE)
        o_ref[0, :, pl.ds(col0, PAGE_SIZE)] = page_scores(page_refs[j][0])

    @pl.when(g == 0)
    def _():
        past = o_ref.shape[2] - PAGE_SIZE
        qpos = lax.broadcasted_iota(jnp.int32, (t, PAGE_SIZE), 0)
        kpos = lax.broadcasted_iota(jnp.int32, (t, PAGE_SIZE), 1)
        o_ref[0, :, past:] = jnp.where(kpos <= qpos, page_scores(kin_ref[0]), -jnp.inf)


def _sample_scores(page_table, qi_s, w_s, ki_new_t, cache_kidx_t):
    B, n_pages = page_table.shape
    t = qi_s.shape[1] // IDX_HEADS
    past = n_pages * PAGE_SIZE
    per_b = lambda b, g, pt: (b, 0, 0)
    page_specs = [pl.BlockSpec((1, IDX_DIM, PAGE_SIZE),
                               functools.partial(lambda b, g, pt, j: (pt[b, g * SCORE_PAGES + j], 0, 0), j=j))
                  for j in range(SCORE_PAGES)]
    return pl.pallas_call(
        _sample_scores_kernel,
        grid_spec=pltpu.PrefetchScalarGridSpec(
            num_scalar_prefetch=1,
            grid=(B, n_pages // SCORE_PAGES),
            in_specs=[pl.BlockSpec((1, IDX_HEADS * t, IDX_DIM), per_b), pl.BlockSpec((1, IDX_HEADS * t, 1), per_b),
                      pl.BlockSpec((1, IDX_DIM, PAGE_SIZE), per_b)] + page_specs,
            out_specs=pl.BlockSpec((1, t, past + PAGE_SIZE), per_b)),
        out_shape=jax.ShapeDtypeStruct((B, t, past + PAGE_SIZE), F32),
        compiler_params=_cparams(("arbitrary", "arbitrary")),
        name="sample_scores",
    )(page_table, qi_s, w_s, ki_new_t, *([cache_kidx_t] * SCORE_PAGES))


def _select_kernel(sc_ref, tau_ref, jl_ref, *, n_keep):
    tau_ref[...] = _select_rows(sc_ref, sc_ref.shape[1] // SEL_CW, SEL_CW, n_keep, jl_ref)


def _sample_select(scores2d, n_keep):
    r, l = scores2d.shape
    rows = min(SEL_ROWS, r)
    row = lambda i: (i, 0)
    return pl.pallas_call(
        functools.partial(_select_kernel, n_keep=n_keep),
        grid=(r // rows,),
        in_specs=[pl.BlockSpec((rows, l), row)],
        out_specs=[pl.BlockSpec((rows, LANES), row), pl.BlockSpec((rows, LANES), row)],
        out_shape=[jax.ShapeDtypeStruct((r, LANES), F32), jax.ShapeDtypeStruct((r, LANES), F32)],
        compiler_params=_cparams(("arbitrary",)),
        name="sample_select",
    )(scores2d)


def _sample_attn_kernel(pt_ref, q_ref, sc_ref, tau_ref, jl_ref, kn_ref, vn_ref, *rest):
    del pt_ref
    k_refs, v_refs = rest[:ATTN_PAGES], rest[ATTN_PAGES:2 * ATTN_PAGES]
    o_ref, m_ref, l_ref, acc_ref, kbuf_ref, vbuf_ref = rest[2 * ATTN_PAGES:]
    g = pl.program_id(1)
    t = sc_ref.shape[1]
    group = N_HEADS // N_KV_HEADS
    span = ATTN_PAGES * PAGE_SIZE

    @pl.when(g == 0)
    def _():
        m_ref[...] = jnp.full(m_ref.shape, NEG_BIG, F32)
        l_ref[...] = jnp.zeros(l_ref.shape, F32)
        acc_ref[...] = jnp.zeros(acc_ref.shape, F32)

    tau = tau_ref[0][:, 0:1]
    jl = jl_ref[0][:, 0:1]

    def attend(k_of, v_of, x, col0f):
        col = lax.broadcasted_iota(jnp.int32, x.shape, 1).astype(F32) + col0f
        sel = (x > tau) | ((x == tau) & (col <= jl))
        sel = jnp.concatenate([jnp.where(sel, 1.0, 0.0)] * group, axis=0) > 0.5
        for kv in range(N_KV_HEADS):
            rows = slice(kv * group * t, (kv + 1) * group * t)
            s = jnp.dot(q_ref[0, rows, :], k_of(kv), preferred_element_type=F32)
            s = jnp.where(sel, s, NEG_BIG)
            m_old = m_ref[rows]
            m_new = jnp.maximum(m_old, _lane_bcast(jnp.max(s, axis=-1, keepdims=True)))
            alpha = jnp.exp(m_old - m_new)
            p = jnp.exp(s - m_new[:, 0:1])
            l_ref[rows] = alpha * l_ref[rows] + _lane_bcast(jnp.sum(p, axis=-1, keepdims=True))
            pv = lax.dot_general(p.astype(BF16), v_of(kv), _NT, preferred_element_type=F32)
            acc_ref[rows] = alpha[:, :HEAD_DIM] * acc_ref[rows] + pv
            m_ref[rows] = m_new

    for j in range(ATTN_PAGES):
        kbuf_ref[:, :, j * PAGE_SIZE:(j + 1) * PAGE_SIZE] = k_refs[j][0].astype(BF16)
        vbuf_ref[:, :, j * PAGE_SIZE:(j + 1) * PAGE_SIZE] = v_refs[j][0].astype(BF16)
    col0 = pl.multiple_of(g * span, span)
    attend(lambda kv: kbuf_ref[kv], lambda kv: vbuf_ref[kv], sc_ref[0, :, pl.ds(col0, span)], col0.astype(F32))

    @pl.when(g == pl.num_programs(1) - 1)
    def _():
        past = sc_ref.shape[2] - PAGE_SIZE
        attend(lambda kv: kn_ref[0, kv], lambda kv: vn_ref[0, kv], sc_ref[0, :, past:], float(past))
        o_ref[0] = acc_ref[...] / l_ref[:, :HEAD_DIM]


def _sample_attn(page_table, q_s, scores, tau, jl, k_new_t, v_new_t, cache_k_t, cache_v_t):
    B, n_pages = page_table.shape
    t = scores.shape[1]
    rows = N_HEADS * t
    per_b3 = lambda b, g, pt: (b, 0, 0)
    per_b4 = lambda b, g, pt: (b, 0, 0, 0)
    page_blk = (1, N_KV_HEADS, HEAD_DIM, PAGE_SIZE)

    def page_specs():
        return [pl.BlockSpec(page_blk,
                             functools.partial(lambda b, g, pt, j: (pt[b, g * ATTN_PAGES + j], 0, 0, 0), j=j))
                for j in range(ATTN_PAGES)]

    return pl.pallas_call(
        _sample_attn_kernel,
        grid_spec=pltpu.PrefetchScalarGridSpec(
            num_scalar_prefetch=1,
            grid=(B, n_pages // ATTN_PAGES),
            in_specs=[pl.BlockSpec((1, rows, HEAD_DIM), per_b3), pl.BlockSpec((1, t, scores.shape[2]), per_b3),
                      pl.BlockSpec((1, t, LANES), per_b3), pl.BlockSpec((1, t, LANES), per_b3),
                      pl.BlockSpec(page_blk, per_b4), pl.BlockSpec(page_blk, per_b4)]
                     + page_specs() + page_specs(),
            out_specs=pl.BlockSpec((1, rows, HEAD_DIM), per_b3),
            scratch_shapes=[pltpu.VMEM((rows, LANES), F32)] * 2 + [pltpu.VMEM((rows, HEAD_DIM), F32)]
                           + [pltpu.VMEM((N_KV_HEADS, HEAD_DIM, ATTN_PAGES * PAGE_SIZE), BF16)] * 2),
        out_shape=jax.ShapeDtypeStruct((B, rows, HEAD_DIM), F32),
        compiler_params=_cparams(("arbitrary", "arbitrary")),
        name="sample_attn",
    )(page_table, q_s, scores, tau, jl, k_new_t, v_new_t,
      *([cache_k_t] * ATTN_PAGES), *([cache_v_t] * ATTN_PAGES))


def _sample_attention(sr, B, T, page_table, cache_k_l, cache_v_l, cache_kidx_l):
    n_pages = page_table.shape[1]
    n_keep = min(TOPK_MAX, (n_pages * PAGE_SIZE + T) // 4)

    def head_major(a, lane_offset):
        a = a.reshape(B, T, N_HEADS, LANES)
        heads = [a[:, :, h, lane_offset(h):lane_offset(h) + HEAD_DIM] for h in range(N_HEADS)]
        return jnp.stack(heads, axis=1).reshape(B, N_HEADS * T, HEAD_DIM)

    def new_transposed(a, heads):
        a = jnp.transpose(a.reshape(B, T, heads, -1), (0, 2, 3, 1))
        return jnp.pad(a, ((0, 0), (0, 0), (0, 0), (0, PAGE_SIZE - T)))

    kidx_t = jnp.transpose(cache_kidx_l, (0, 2, 1))
    k_t = jnp.transpose(cache_k_l, (0, 2, 3, 1))
    v_t = jnp.transpose(cache_v_l, (0, 2, 3, 1))

    qi_s = head_major(sr["qi"], lambda h: 0)
    w_s = jnp.transpose(sr["kiw"][:, IDX_DIM:IDX_DIM + IDX_HEADS].reshape(B, T, IDX_HEADS), (0, 2, 1))
    w_s = w_s.reshape(B, IDX_HEADS * T, 1)
    scores = _sample_scores(page_table, qi_s, w_s, new_transposed(sr["kiw"][:, :IDX_DIM], 1)[:, 0], kidx_t)
    tau, jl = _sample_select(scores.reshape(B * T, -1), n_keep)
    o = _sample_attn(page_table, head_major(sr["q"], _kv_lane_offset), scores, tau.reshape(B, T, LANES),
                     jl.reshape(B, T, LANES), new_transposed(sr["kb"], N_KV_HEADS),
                     new_transposed(sr["v"].astype(BF16), N_KV_HEADS), k_t, v_t)
    o = jnp.transpose(o.reshape(B, N_HEADS, T, HEAD_DIM), (0, 2, 1, 3))
    return o.reshape(B * T, N_HEADS * HEAD_DIM).astype(BF16)


def _layer(x, mods, conv_past, attn_fn, p, tiles):
    B, T, _ = x.shape
    sh1, sc1, g1, sh2, sc2, g2 = mods
    pr = _inproj(x, sc1, sh1, p["norm1_g"], p["w_pack"], p["q_norm_g"], p["k_norm_g"], *tiles["inproj"])
    cin = pr["cin"].reshape(B, T, CONV_CH)
    if conv_past is None:
        yc = _conv_prompt(pr["cin"], B, T, p["conv_w"], p["conv_b"], p["ln_g"], p["ln_b"])
        conv_state = cin[:, T - (CONV_W - 1):]
    else:
        cpad = jnp.concatenate([conv_past, cin], axis=1)
        yc = _conv_sample(cpad, p["conv_w"], p["conv_b"], p["ln_g"], p["ln_b"]).reshape(B * T, CONV_CH)
        conv_state = cpad[:, T:]
    attn = attn_fn(pr)
    x1, h2, gate = _merge(x, g1, sc2, sh2, yc, attn, pr["gates"], p["w_conv_out"], p["w_attn_o"], p["w_out"],
                          p["norm2_g"], p["w_grp"], p["b_grp"], p["w_exp"], p["b_exp"], *tiles["merge"])
    y = _moe(h2, gate, x1, g2, p["w13"], p["w2b"], *tiles["moe"])
    k = pr["k"].reshape(B, T, N_KV_HEADS, HEAD_DIM)
    v = pr["v"].reshape(B, T, N_KV_HEADS, HEAD_DIM)
    ki = pr["kiw"][:, :IDX_DIM].reshape(B, T, IDX_DIM)
    return y, k, v, ki, conv_state


def kernel(x_prompt, x_sample, cache_k, cache_v, cache_kidx, state_conv, page_table, c_prompt, c_sample,
           w_ada, b_ada, norm1_g, w_in, q_norm_g, k_norm_g, conv_w, conv_b, ln_g, ln_b, w_conv_out, w_attn_o,
           w_out, norm2_g, w_grp, b_grp, w_exp, b_exp, w1, w3, w2):
    depth = w_ada.shape[0]
    Bp, S, _ = x_prompt.shape
    Bs, T, _ = x_sample.shape
    xp, xs = x_prompt, x_sample
    outs = [[] for _ in range(8)]
    c_all = jnp.concatenate([c_prompt, c_sample], axis=0)
    c_all = jnp.pad(c_all, ((0, (-c_all.shape[0]) % 16), (0, 0)))
    tiles_p = {"inproj": (1, min(256, S)), "merge": (1, min(512, S)), "moe": (1, min(1024, S))}
    tiles_s = {"inproj": (min(32, Bs), T), "merge": (min(64, Bs), T), "moe": (min(128, Bs), T)}
    for l in range(depth):
        p = dict(norm1_g=norm1_g[l], w_pack=_pack_w_in(w_in[l]), q_norm_g=q_norm_g[l], k_norm_g=k_norm_g[l],
                 conv_w=conv_w[l], conv_b=conv_b[l], ln_g=ln_g[l], ln_b=ln_b[l], w_conv_out=w_conv_out[l],
                 w_attn_o=w_attn_o[l], w_out=w_out[l], norm2_g=norm2_g[l], w_grp=w_grp[l], b_grp=b_grp[l],
                 w_exp=w_exp[l], b_exp=b_exp[l],
                 w13=jnp.concatenate([w1[l], w3[l]], axis=-1).astype(BF16), w2b=w2[l].astype(BF16))
        mod = _ada(c_all, w_ada[l], b_ada[l])
        mods = [m[:, None, :] for m in jnp.split(mod, 6, axis=-1)]
        mods_p = [m[:Bp] for m in mods]
        mods_s = [m[Bp:Bp + Bs] for m in mods]

        attn_p = lambda pr: _attn_prompt(pr["q"], pr["qi"], pr["kiw"], pr["kib"], pr["kb"], pr["vb"], Bp, S)
        xp, k, v, ki, cv = _layer(xp, mods_p, None, attn_p, p, tiles_p)
        for o, a in zip(outs[:4], (k, v, ki, cv)):
            o.append(a)

        attn_s = lambda pr: _sample_attention(pr, Bs, T, page_table, cache_k[l], cache_v[l], cache_kidx[l])
        xs, k, v, ki, cv = _layer(xs, mods_s, state_conv[l], attn_s, p, tiles_s)
        for o, a in zip(outs[4:], (k, v, ki, cv)):
            o.append(a)
    return (xp, xs) + tuple(jnp.stack(o) for o in outs)
```

```python
import functools

import jax
import jax.numpy as jnp
from jax import lax
from jax.experimental import pallas as pl
from jax.experimental.pallas import tpu as pltpu

F32 = jnp.float32
BF16 = jnp.bfloat16

D_MODEL = 1024
CONV_CH = 512
CONV_W = 31
N_HEADS = 8
HEAD_DIM = 64
N_KV_HEADS = 4
KV_W = N_KV_HEADS * HEAD_DIM
IDX_HEADS = 8
IDX_DIM = 64
IDX_SCALE = (IDX_HEADS * IDX_DIM) ** -0.5
TOPK_MAX = 256
PAGE_SIZE = 128
N_GROUPS = 4
EXP_PER_GROUP = 8
N_EXPERTS = 32
EXPERT_FF = 256
EPS = 1e-6

LANES = 128
SUBLANES = 8
LOG2E = 1.4426950408889634
NEG_BIG = -1e30
VMEM_LIMIT = 56 * 1024 * 1024

C_GLU = 0
C_Q = C_GLU + 2 * CONV_CH
C_K = C_Q + N_HEADS * LANES
C_V = C_K + KV_W
C_QI = C_V + KV_W
C_KIW = C_QI + IDX_HEADS * LANES
C_GATES = C_KIW + LANES
IN_PACK = C_GATES + 2 * D_MODEL

N_BISECT = 16


def _cparams(sem):
    return pltpu.CompilerParams(dimension_semantics=sem, vmem_limit_bytes=VMEM_LIMIT)


def _sigmoid(x):
    return 1.0 / (1.0 + jnp.exp(-x))


def _silu(x):
    return x * _sigmoid(x)


def _const_spec(shape):
    nd = len(shape)
    return pl.BlockSpec(shape, lambda *_: (0,) * nd)


def _ada_kernel(c_ref, w_ref, b_ref, o_ref):
    s = _silu(c_ref[...])
    o_ref[...] = jnp.dot(s.astype(BF16), w_ref[...], preferred_element_type=F32) + b_ref[...]


def _ada(c, w_ada, b_ada):
    r = c.shape[0]
    tn = 1536
    return pl.pallas_call(
        _ada_kernel,
        grid=(6 * D_MODEL // tn,),
        in_specs=[pl.BlockSpec((r, D_MODEL), lambda j: (0, 0)),
                  pl.BlockSpec((D_MODEL, tn), lambda j: (0, j)),
                  pl.BlockSpec((1, tn), lambda j: (0, j))],
        out_specs=pl.BlockSpec((r, tn), lambda j: (0, j)),
        out_shape=jax.ShapeDtypeStruct((r, 6 * D_MODEL), F32),
        compiler_params=_cparams(("arbitrary",)),
        name="adaln_mod",
    )(c, w_ada.astype(BF16), b_ada.reshape(1, -1))


def _inproj_kernel(x_ref, sc_ref, sh_ref, g_ref, w_ref, qg_ref, kg_ref, ones_ref,
                   cin_ref, q_ref, k_ref, v_ref, kb_ref, vb_ref, qi_ref, kiw_ref, kib_ref, gates_ref):
    bb, tt, _ = x_ref.shape
    x = x_ref[...]
    ms = jnp.mean(x * x, axis=-1, keepdims=True)
    h = x * lax.rsqrt(ms + EPS) * g_ref[...] * (1.0 + sc_ref[...]) + sh_ref[...]
    hb = h.reshape(bb * tt, D_MODEL).astype(BF16)

    def proj(c0, width):
        return jnp.dot(hb, w_ref[:, c0:c0 + width], preferred_element_type=F32)

    glu = proj(C_GLU, 2 * CONV_CH)
    cin_ref[...] = glu[:, :CONV_CH] * _sigmoid(glu[:, CONV_CH:])

    for hd in range(N_HEADS):
        if hd % 2 == 0:
            q2 = proj(C_Q + hd * LANES, 2 * LANES)
        qh = q2[:, (hd % 2) * LANES:(hd % 2 + 1) * LANES]
        msq = jnp.sum(qh * qh, axis=-1, keepdims=True) * (1.0 / HEAD_DIM)
        qn = qh * lax.rsqrt(msq + EPS) * qg_ref[:, hd * LANES:(hd + 1) * LANES]
        q_ref[:, hd * LANES:(hd + 1) * LANES] = (qn * (HEAD_DIM ** -0.5 * LOG2E)).astype(BF16)

    k = proj(C_K, KV_W)
    k2 = k * k
    k2_hi = k2.astype(BF16)
    k2_lo = (k2 - k2_hi.astype(F32)).astype(BF16)
    ksum = (jnp.dot(k2_hi, ones_ref[...], preferred_element_type=F32)
            + jnp.dot(k2_lo, ones_ref[...], preferred_element_type=F32))
    kn = k * lax.rsqrt(ksum * (1.0 / HEAD_DIM) + EPS) * kg_ref[...]
    k_ref[...] = kn
    kb_ref[...] = kn.astype(BF16)

    v = proj(C_V, KV_W)
    v_ref[...] = v
    ones_blk = jnp.ones((v.shape[0], LANES), BF16)
    for pair in range(N_KV_HEADS // 2):
        vb_ref[:, pair * 2 * LANES:pair * 2 * LANES + LANES] = v[:, pair * LANES:(pair + 1) * LANES].astype(BF16)
        vb_ref[:, pair * 2 * LANES + LANES:(pair + 1) * 2 * LANES] = ones_blk

    qi_ref[...] = proj(C_QI, IDX_HEADS * LANES).astype(BF16)

    kiw = proj(C_KIW, LANES)
    kiw_ref[...] = kiw
    lane = lax.broadcasted_iota(jnp.int32, kiw.shape, 1)
    kib_ref[...] = jnp.where(lane < IDX_DIM, kiw, 0.0).astype(BF16)

    gates_ref[...] = _sigmoid(proj(C_GATES, 2 * D_MODEL))


def _pack_w_in(w_in):
    o = 0
    segs = {}
    for name, width in (("glu", 2 * CONV_CH), ("q", N_HEADS * HEAD_DIM), ("k", KV_W), ("v", KV_W),
                        ("qi", IDX_HEADS * IDX_DIM), ("ki", IDX_DIM), ("wi", IDX_HEADS), ("gates", 2 * D_MODEL)):
        segs[name] = w_in[:, o:o + width]
        o += width
    zeros64 = jnp.zeros((D_MODEL, HEAD_DIM), F32)
    cols = [segs["glu"]]
    for hd in range(N_HEADS):
        wq = segs["q"][:, hd * HEAD_DIM:(hd + 1) * HEAD_DIM]
        cols += [zeros64, wq] if (hd // 2) % 2 else [wq, zeros64]
    cols += [segs["k"], segs["v"]]
    for hd in range(IDX_HEADS):
        cols += [segs["qi"][:, hd * IDX_DIM:(hd + 1) * IDX_DIM], zeros64]
    cols += [segs["ki"], segs["wi"], jnp.zeros((D_MODEL, LANES - IDX_DIM - IDX_HEADS), F32)]
    cols += [segs["gates"]]
    return jnp.concatenate(cols, axis=1).astype(BF16)


def _pack_q_gain(q_norm_g):
    z = jnp.zeros((HEAD_DIM,), F32)
    parts = []
    for hd in range(N_HEADS):
        parts += [z, q_norm_g] if (hd // 2) % 2 else [q_norm_g, z]
    return jnp.concatenate(parts).reshape(1, N_HEADS * LANES)


def _inproj(x, sc1, sh1, norm1_g, w_pack, q_norm_g, k_norm_g, bb, tt):
    B, T, _ = x.shape
    n = B * T
    tm = bb * tt
    nb_t = T // tt
    grid = (B // bb, nb_t)
    row = lambda i, j: (i * nb_t + j, 0)
    ones = (lax.broadcasted_iota(jnp.int32, (KV_W, KV_W), 0) // HEAD_DIM
            == lax.broadcasted_iota(jnp.int32, (KV_W, KV_W), 1) // HEAD_DIM).astype(BF16)
    outs = [("cin", CONV_CH, F32), ("q", N_HEADS * LANES, BF16), ("k", KV_W, F32), ("v", KV_W, F32),
            ("kb", KV_W, BF16), ("vb", 2 * KV_W, BF16), ("qi", IDX_HEADS * LANES, BF16), ("kiw", LANES, F32),
            ("kib", LANES, BF16), ("gates", 2 * D_MODEL, F32)]
    res = pl.pallas_call(
        _inproj_kernel,
        grid=grid,
        in_specs=[pl.BlockSpec((bb, tt, D_MODEL), lambda i, j: (i, j, 0)),
                  pl.BlockSpec((bb, 1, D_MODEL), lambda i, j: (i, 0, 0)),
                  pl.BlockSpec((bb, 1, D_MODEL), lambda i, j: (i, 0, 0)),
                  _const_spec((1, 1, D_MODEL)),
                  _const_spec((D_MODEL, IN_PACK)),
                  _const_spec((1, N_HEADS * LANES)),
                  _const_spec((1, KV_W)),
                  _const_spec((KV_W, KV_W))],
        out_specs=[pl.BlockSpec((tm, w), row) for _, w, _ in outs],
        out_shape=[jax.ShapeDtypeStruct((n, w), dt) for _, w, dt in outs],
        compiler_params=_cparams(("arbitrary", "arbitrary")),
        name="in_proj",
    )(x, sc1, sh1, norm1_g.reshape(1, 1, D_MODEL), w_pack, _pack_q_gain(q_norm_g),
      jnp.tile(k_norm_g, N_KV_HEADS).reshape(1, KV_W), ones)
    return {name: r for (name, _, _), r in zip(outs, res)}


CONV_HALO = 32
CONV_ROWS = 32


def _ln_silu(y, g, b):
    mu = jnp.mean(y, axis=-1, keepdims=True)
    yc = y - mu
    var = jnp.mean(yc * yc, axis=-1, keepdims=True)
    return _silu(yc * lax.rsqrt(var + EPS) * g + b)


def _conv_prompt_kernel(cin_ref, w_ref, b_ref, g_ref, lb_ref, o_ref, win_ref, sh_ref):
    tc = cin_ref.shape[0]
    span = CONV_HALO + tc

    @pl.when(pl.program_id(1) == 0)
    def _():
        win_ref[0:CONV_HALO, :] = jnp.zeros((CONV_HALO, CONV_CH), F32)

    win_ref[CONV_HALO:span, :] = cin_ref[...]
    win_ref[span:, :] = jnp.zeros((SUBLANES, CONV_CH), F32)
    for r in range(SUBLANES):
        sh_ref[r] = win_ref[r:r + span, :]
    off = CONV_HALO - (CONV_W - 1)
    for r0 in range(0, tc, CONV_ROWS):
        acc = jnp.zeros((CONV_ROWS, CONV_CH), F32) + b_ref[...]
        for j in range(CONV_W):
            r = (off + j) % SUBLANES
            a = r0 + off + j - r
            acc = acc + sh_ref[r, a:a + CONV_ROWS, :] * w_ref[j:j + 1, :]
        o_ref[r0:r0 + CONV_ROWS, :] = _ln_silu(acc, g_ref[...], lb_ref[...]).astype(BF16)
    win_ref[0:CONV_HALO, :] = cin_ref[tc - CONV_HALO:, :]


def _conv_prompt(cin, B, S, conv_w, conv_b, ln_g, ln_b):
    tc = 256
    nt = S // tc
    return pl.pallas_call(
        _conv_prompt_kernel,
        grid=(B, nt),
        in_specs=[pl.BlockSpec((tc, CONV_CH), lambda b, j: (b * nt + j, 0)),
                  _const_spec((CONV_W, CONV_CH)), _const_spec((1, CONV_CH)),
                  _const_spec((1, CONV_CH)), _const_spec((1, CONV_CH))],
        out_specs=pl.BlockSpec((tc, CONV_CH), lambda b, j: (b * nt + j, 0)),
        out_shape=jax.ShapeDtypeStruct((B * S, CONV_CH), BF16),
        scratch_shapes=[pltpu.VMEM((CONV_HALO + tc + SUBLANES, CONV_CH), F32),
                        pltpu.VMEM((SUBLANES, CONV_HALO + tc, CONV_CH), F32)],
        compiler_params=_cparams(("arbitrary", "arbitrary")),
        name="conv_prompt",
    )(cin, conv_w, conv_b.reshape(1, -1), ln_g.reshape(1, -1), ln_b.reshape(1, -1))


def _conv_sample_kernel(cpad_ref, w_ref, b_ref, g_ref, lb_ref, o_ref):
    bb, tp, _ = cpad_ref.shape
    t = tp - (CONV_W - 1)
    for b in range(bb):
        acc = jnp.zeros((t, CONV_CH), F32) + b_ref[...]
        for j in range(CONV_W):
            acc = acc + cpad_ref[b, j:j + t, :] * w_ref[j:j + 1, :]
        o_ref[b] = _ln_silu(acc, g_ref[...], lb_ref[...]).astype(BF16)


def _conv_sample(cpad, conv_w, conv_b, ln_g, ln_b):
    B, tp, _ = cpad.shape
    t = tp - (CONV_W - 1)
    bb = 8
    return pl.pallas_call(
        _conv_sample_kernel,
        grid=(B // bb,),
        in_specs=[pl.BlockSpec((bb, tp, CONV_CH), lambda i: (i, 0, 0)),
                  _const_spec((CONV_W, CONV_CH)), _const_spec((1, CONV_CH)),
                  _const_spec((1, CONV_CH)), _const_spec((1, CONV_CH))],
        out_specs=pl.BlockSpec((bb, t, CONV_CH), lambda i: (i, 0, 0)),
        out_shape=jax.ShapeDtypeStruct((B, t, CONV_CH), BF16),
        compiler_params=_cparams(("arbitrary",)),
        name="conv_sample",
    )(cpad, conv_w, conv_b.reshape(1, -1), ln_g.reshape(1, -1), ln_b.reshape(1, -1))


R_EXP = 0
R_GRP = N_EXPERTS


def _lane_min_index(mask, lane):
    return jnp.min(jnp.where(mask, lane, float(LANES)), axis=-1, keepdims=True)


def _router_gate(logits):
    lane = lax.broadcasted_iota(jnp.int32, logits.shape, 1).astype(F32)
    is_grp = (lane >= R_GRP) & (lane < R_GRP + N_GROUPS)
    gl = jnp.where(is_grp, logits, -jnp.inf)
    gmax = jnp.max(gl, axis=-1, keepdims=True)
    gexp = jnp.where(is_grp, jnp.exp(gl - gmax), 0.0)
    p_grp = gexp / jnp.sum(gexp, axis=-1, keepdims=True)
    g_top = jnp.max(p_grp, axis=-1, keepdims=True)
    g_idx = _lane_min_index(is_grp & (p_grp == g_top), lane) - R_GRP
    in_grp = (lane >= g_idx * EXP_PER_GROUP) & (lane < (g_idx + 1) * EXP_PER_GROUP)
    el = jnp.where(in_grp, logits, -jnp.inf)
    emax = jnp.max(el, axis=-1, keepdims=True)
    eexp = jnp.where(in_grp, jnp.exp(el - emax), 0.0)
    p_exp = eexp / jnp.sum(eexp, axis=-1, keepdims=True)
    p1 = jnp.max(jnp.where(in_grp, p_exp, -1.0), axis=-1, keepdims=True)
    i1 = _lane_min_index(in_grp & (p_exp == p1), lane)
    rest = in_grp & (lane != i1)
    p2 = jnp.max(jnp.where(rest, p_exp, -1.0), axis=-1, keepdims=True)
    i2 = _lane_min_index(rest & (p_exp == p2), lane)
    tot = p1 + p2
    return jnp.where(lane == i1, g_top * p1 / tot, 0.0) + jnp.where(lane == i2, g_top * p2 / tot, 0.0)


def _merge_kernel(x_ref, g1_ref, sc2_ref, sh2_ref, yc_ref, at_ref, gates_ref, wco_ref, wao_ref, wout_ref,
                  n2g_ref, wrh_ref, wrl_ref, br_ref, x1_ref, h2_ref, gate_ref):
    bb, tt, _ = x_ref.shape
    y_conv = jnp.dot(yc_ref[...], wco_ref[...], preferred_element_type=F32)
    y_attn = jnp.dot(at_ref[...], wao_ref[...], preferred_element_type=F32)
    m = gates_ref[:, :D_MODEL] * y_conv + gates_ref[:, D_MODEL:] * y_attn
    mo = jnp.dot(m.astype(BF16), wout_ref[...], preferred_element_type=F32)
    x1 = x_ref[...] + g1_ref[...] * mo.reshape(bb, tt, D_MODEL)
    x1_ref[...] = x1
    ms = jnp.mean(x1 * x1, axis=-1, keepdims=True)
    h2 = (x1 * lax.rsqrt(ms + EPS) * n2g_ref[...] * (1.0 + sc2_ref[...]) + sh2_ref[...]).reshape(bb * tt, D_MODEL)
    h2_hi = h2.astype(BF16)
    h2_ref[...] = h2_hi
    h2_lo = (h2 - h2_hi.astype(F32)).astype(BF16)
    logits = (jnp.dot(h2_hi, wrh_ref[...], preferred_element_type=F32)
              + jnp.dot(h2_lo, wrh_ref[...], preferred_element_type=F32)
              + jnp.dot(h2_hi, wrl_ref[...], preferred_element_type=F32)) + br_ref[...]
    gate_ref[...] = _router_gate(logits)


def _merge(x, g1, sc2, sh2, yc, attn, gates, w_conv_out, w_attn_o, w_out, norm2_g, w_grp, b_grp, w_exp, b_exp,
           bb, tt):
    B, T, _ = x.shape
    n = B * T
    tm = bb * tt
    nb_t = T // tt
    row = lambda i, j: (i * nb_t + j, 0)
    mod = lambda i, j: (i, 0, 0)
    pad = jnp.zeros((D_MODEL, LANES - N_EXPERTS - N_GROUPS), F32)
    wr = jnp.concatenate([w_exp, w_grp, pad], axis=1)
    wr_hi = wr.astype(BF16)
    br = jnp.concatenate([b_exp, b_grp, jnp.zeros((LANES - N_EXPERTS - N_GROUPS,), F32)]).reshape(1, LANES)
    return pl.pallas_call(
        _merge_kernel,
        grid=(B // bb, nb_t),
        in_specs=[pl.BlockSpec((bb, tt, D_MODEL), lambda i, j: (i, j, 0)),
                  pl.BlockSpec((bb, 1, D_MODEL), mod), pl.BlockSpec((bb, 1, D_MODEL), mod),
                  pl.BlockSpec((bb, 1, D_MODEL), mod),
                  pl.BlockSpec((tm, CONV_CH), row), pl.BlockSpec((tm, N_HEADS * HEAD_DIM), row),
                  pl.BlockSpec((tm, 2 * D_MODEL), row),
                  _const_spec((CONV_CH, D_MODEL)), _const_spec((N_HEADS * HEAD_DIM, D_MODEL)),
                  _const_spec((D_MODEL, D_MODEL)), _const_spec((1, 1, D_MODEL)),
                  _const_spec((D_MODEL, LANES)), _const_spec((D_MODEL, LANES)), _const_spec((1, LANES))],
        out_specs=[pl.BlockSpec((bb, tt, D_MODEL), lambda i, j: (i, j, 0)),
                   pl.BlockSpec((tm, D_MODEL), row), pl.BlockSpec((tm, LANES), row)],
        out_shape=[jax.ShapeDtypeStruct((B, T, D_MODEL), F32), jax.ShapeDtypeStruct((n, D_MODEL), BF16),
                   jax.ShapeDtypeStruct((n, LANES), F32)],
        compiler_params=_cparams(("arbitrary", "arbitrary")),
        name="merge_router",
    )(x, g1, sc2, sh2, yc, attn, gates, w_conv_out.astype(BF16), w_attn_o.astype(BF16), w_out.astype(BF16),
      norm2_g.reshape(1, 1, D_MODEL), wr_hi, (wr - wr_hi.astype(F32)).astype(BF16), br)


def _moe_kernel(h2_ref, gate_ref, w13_ref, w2_ref, x1_ref, g2_ref, o_ref, acc_ref):
    e = pl.program_id(2)
    bb, tt, _ = x1_ref.shape

    @pl.when(e == 0)
    def _():
        acc_ref[...] = jnp.zeros_like(acc_ref)

    ab = jnp.dot(h2_ref[...], w13_ref[0], preferred_element_type=F32)
    lane = lax.broadcasted_iota(jnp.int32, gate_ref.shape, 1)
    gcol = jnp.sum(jnp.where(lane == e, gate_ref[...], 0.0), axis=-1, keepdims=True)
    hid = _silu(ab[:, :EXPERT_FF]) * ab[:, EXPERT_FF:] * gcol
    acc_ref[...] += jnp.dot(hid.astype(BF16), w2_ref[0], preferred_element_type=F32)

    @pl.when(e == N_EXPERTS - 1)
    def _():
        o_ref[...] = x1_ref[...] + g2_ref[...] * acc_ref[...].reshape(bb, tt, D_MODEL)


def _moe(h2, gate, x1, g2, w13, w2b, bb, tt):
    B, T, _ = x1.shape
    tm = bb * tt
    nb_t = T // tt
    row = lambda i, j, e: (i * nb_t + j, 0)
    return pl.pallas_call(
        _moe_kernel,
        grid=(B // bb, nb_t, N_EXPERTS),
        in_specs=[pl.BlockSpec((tm, D_MODEL), row), pl.BlockSpec((tm, LANES), row),
                  pl.BlockSpec((1, D_MODEL, 2 * EXPERT_FF), lambda i, j, e: (e, 0, 0)),
                  pl.BlockSpec((1, EXPERT_FF, D_MODEL), lambda i, j, e: (e, 0, 0)),
                  pl.BlockSpec((bb, tt, D_MODEL), lambda i, j, e: (i, j, 0)),
                  pl.BlockSpec((bb, 1, D_MODEL), lambda i, j, e: (i, 0, 0))],
        out_specs=pl.BlockSpec((bb, tt, D_MODEL), lambda i, j, e: (i, j, 0)),
        out_shape=jax.ShapeDtypeStruct((B, T, D_MODEL), F32),
        scratch_shapes=[pltpu.VMEM((tm, D_MODEL), F32)],
        compiler_params=_cparams(("arbitrary", "arbitrary", "arbitrary")),
        name="moe_dense",
    )(h2, gate, w13, w2b, x1, g2)


def _lane_bcast(col):
    return jnp.broadcast_to(col, (col.shape[0], LANES))


def _row_any(flag_rows):
    return jnp.max(jnp.where(flag_rows, 1.0, 0.0)) > 0.0


def _select_rows(sc_ref, nch, cw, n_keep, jl_ref):
    R = sc_ref.shape[0]
    ng = cw // LANES
    kf = float(n_keep)
    inf = jnp.full((R, LANES), jnp.inf, F32)
    zero = jnp.zeros((R, LANES), F32)
    lane = lax.broadcasted_iota(jnp.int32, (R, LANES), 1).astype(F32)

    def scan(init, fn):
        def body(c, carry):
            base = pl.multiple_of(c * cw, LANES)
            for j in range(ng):
                blk = sc_ref[:, pl.ds(base + j * LANES, LANES)]
                carry = fn(carry, blk, (base + j * LANES).astype(F32))
            return carry
        return lax.fori_loop(0, nch, body, init)

    rsum = lambda a: _lane_bcast(jnp.sum(a, axis=-1, keepdims=True))
    rmin = lambda a: _lane_bcast(jnp.min(a, axis=-1, keepdims=True))
    rmax = lambda a: _lane_bcast(jnp.max(a, axis=-1, keepdims=True))

    mx, mn = scan((-inf, inf), lambda c, x, _: (jnp.maximum(c[0], x),
                                                 jnp.minimum(c[1], jnp.where(x == -jnp.inf, jnp.inf, x))))
    hi0, lo0 = rmax(mx), rmin(mn)

    def bisect(_, lh):
        lo, hi = lh
        mid = 0.5 * lo + 0.5 * hi
        cnt = rsum(scan(zero, lambda c, x, _: c + jnp.where(x >= mid, 1.0, 0.0)))
        ge = cnt >= kf
        return jnp.where(ge, mid, lo), jnp.where(ge, hi, mid)

    lo, _ = lax.fori_loop(0, N_BISECT, bisect, (lo0, hi0))
    tau0 = rmin(scan(inf, lambda c, x, _: jnp.minimum(c, jnp.where(x >= lo, x, jnp.inf))))

    def refine_pass(tau):
        def fn(c, x, _):
            gt = x > tau
            return (c[0] + jnp.where(gt, 1.0, 0.0), jnp.minimum(c[1], jnp.where(gt, x, jnp.inf)),
                    c[2] + jnp.where(x == tau, 1.0, 0.0))
        cgt, nxt, ceq = scan((zero, inf, zero), fn)
        return rsum(cgt), rmin(nxt), rsum(ceq)

    def refine_body(state):
        tau, _, _, _ = state
        cgt, nxt, ceq = refine_pass(tau)
        up = cgt >= kf
        return jnp.where(up, nxt, tau), cgt, ceq, _row_any(up)

    tau, cgt, ceq, _ = lax.while_loop(lambda s: s[3], refine_body, (tau0, zero, zero, jnp.bool_(True)))

    need = kf - cgt
    excess = ceq > need
    jl_ref[...] = inf

    @pl.when(_row_any(excess))
    def _():
        def idx_bisect(_, lh):
            lo_j, hi_j = lh
            mid = jnp.floor((lo_j + hi_j) * 0.5)
            cnt = rsum(scan(zero, lambda c, x, col0: c + jnp.where((x == tau) & (col0 + lane <= mid), 1.0, 0.0)))
            ge = cnt >= need
            return jnp.where(ge, lo_j, mid), jnp.where(ge, mid, hi_j)

        ncol = (nch * cw).astype(F32) if hasattr(nch, "astype") else float(nch * cw)
        _, hi_j = lax.fori_loop(0, 14, idx_bisect, (zero - 1.0, zero + (ncol - 1.0)))
        jl_ref[...] = jnp.where(excess, hi_j, jnp.inf)

    return tau


ATT_TQ = 128
ATT_TK = 512
IDX_TN = 256
SEL_CHUNK = 1024
BIAS_ROWS = 16

_NT = (((1,), (1,)), ((), ()))


def _kv_lane_offset(head):
    return ((head // 2) % 2) * HEAD_DIM


def _assemble_heads(per_head):
    lane = lax.broadcasted_iota(jnp.int32, per_head[0].shape, 1)
    cols = []
    for c in range(N_HEADS // 2):
        a, b = per_head[2 * c], per_head[2 * c + 1]
        if _kv_lane_offset(2 * c) == 0:
            b = pltpu.roll(b, HEAD_DIM, 1)
        else:
            a = pltpu.roll(a, HEAD_DIM, 1)
        cols.append(jnp.where(lane < HEAD_DIM, a, b))
    return jnp.concatenate(cols, axis=1)


def _attn_prompt_kernel(q_ref, qi_ref, kiw_ref, kib_ref, kb_ref, vb_ref, o_ref,
                        sc_ref, wb_ref, jl_ref, tau_ref, bias_ref, m_ref, al_ref, pm_ref, acc_ref, s_ref, p_ref,
                        *, n_keep):
    tq, tk = ATT_TQ, ATT_TK
    qb = pl.program_id(1)
    t0 = qb * tq
    nkb = (t0 + tq + tk - 1) // tk

    for h in range(IDX_HEADS):
        wb_ref[h] = _lane_bcast(kiw_ref[:, IDX_DIM + h:IDX_DIM + h + 1])

    row_g = t0 + lax.broadcasted_iota(jnp.int32, (tq, IDX_TN), 0)
    col_l = lax.broadcasted_iota(jnp.int32, (tq, IDX_TN), 1)

    def score_body(c, _):
        for j in range(tk // IDX_TN):
            c0 = pl.multiple_of(c * tk + j * IDX_TN, IDX_TN)
            kt = kib_ref[0, pl.ds(c0, IDX_TN), :]
            acc = jnp.zeros((tq, IDX_TN), F32)
            for h in range(IDX_HEADS):
                y = lax.dot_general(qi_ref[:, h * LANES:(h + 1) * LANES], kt, _NT, preferred_element_type=F32)
                wb = wb_ref[h]
                acc = acc + jnp.maximum(y, 0.0) * jnp.concatenate([wb] * (IDX_TN // LANES), axis=1)
            sc_ref[:, pl.ds(c0, IDX_TN)] = jnp.where(c0 + col_l <= row_g, acc * IDX_SCALE, -jnp.inf)
        return 0

    lax.fori_loop(0, nkb, score_body, 0)

    per_chunk = SEL_CHUNK // tk
    for k in range(1, per_chunk):
        @pl.when(nkb % per_chunk == k)
        def _():
            for j in range(k, per_chunk):
                sc_ref[:, pl.ds(pl.multiple_of((nkb - k + j) * tk, tk), tk)] = jnp.full((tq, tk), -jnp.inf, F32)

    tau_ref[...] = _select_rows(sc_ref, (nkb + per_chunk - 1) // per_chunk, SEL_CHUNK, n_keep, jl_ref)

    m_ref[...] = jnp.full(m_ref.shape, NEG_BIG, F32)
    acc_ref[...] = jnp.zeros(acc_ref.shape, F32)
    col_k = lax.broadcasted_iota(jnp.int32, (BIAS_ROWS, tk), 1).astype(F32)
    heads_per_pair = N_HEADS // (N_KV_HEADS // 2)
    ncg = tk // LANES

    def tile_lanes(a):
        return jnp.concatenate([a] * ncg, axis=1)

    def attn_body(c, _):
        c0 = pl.multiple_of(c * tk, tk)
        col = col_k + c0.astype(F32)
        for r0 in range(0, tq, BIAS_ROWS):
            rows = slice(r0, r0 + BIAS_ROWS)
            x = sc_ref[rows, pl.ds(c0, tk)]
            tau = tile_lanes(tau_ref[rows, :])
            sel = (x > tau) | ((x == tau) & (col <= tile_lanes(jl_ref[rows, :])))
            bias_ref[rows, :] = jnp.where(sel, 0.0, NEG_BIG)

        for h in range(N_HEADS):
            pair = h // heads_per_pair
            kt = kb_ref[0, pl.ds(c0, tk), pair * LANES:(pair + 1) * LANES]
            s = lax.dot_general(q_ref[:, h * LANES:(h + 1) * LANES], kt, _NT, preferred_element_type=F32)
            s = s + bias_ref[...]
            s_ref[h] = s
            pm = s[:, 0:LANES]
            for j in range(1, ncg):
                pm = jnp.maximum(pm, s[:, j * LANES:(j + 1) * LANES])
            pm_ref[h] = pm

        for h in range(N_HEADS):
            m_old = m_ref[h]
            m_new = jnp.maximum(m_old, _lane_bcast(jnp.max(pm_ref[h], axis=-1, keepdims=True)))
            al_ref[h] = jnp.exp2(m_old - m_new)
            m_ref[h] = m_new

        for h in range(N_HEADS):
            p_ref[h] = jnp.exp2(s_ref[h] - tile_lanes(m_ref[h])).astype(BF16)

        for h in range(N_HEADS):
            pair = h // heads_per_pair
            vt = vb_ref[0, pl.ds(c0, tk), pair * 2 * LANES:(pair + 1) * 2 * LANES]
            alpha = al_ref[h]
            acc_ref[h] = (jnp.concatenate([alpha, alpha], axis=1) * acc_ref[h]
                          + jnp.dot(p_ref[h], vt, preferred_element_type=F32))
        return 0

    lax.fori_loop(0, nkb, attn_body, 0)
    o_ref[...] = _assemble_heads([acc_ref[h, :, :LANES] / acc_ref[h, :, LANES:]
                                  for h in range(N_HEADS)]).astype(BF16)


def _attn_prompt(q, qi, kiw, kib, kb, vb, B, S):
    tq = ATT_TQ
    assert S % SEL_CHUNK == 0 and SEL_CHUNK % ATT_TK == 0 and ATT_TK % tq == 0
    nq = S // tq
    n_keep = min(TOPK_MAX, S // 4)
    row = lambda b, j: (b * nq + j, 0)
    per_b = lambda b, j: (b, 0, 0)
    return pl.pallas_call(
        functools.partial(_attn_prompt_kernel, n_keep=n_keep),
        grid=(B, nq),
        in_specs=[pl.BlockSpec((tq, N_HEADS * LANES), row), pl.BlockSpec((tq, IDX_HEADS * LANES), row),
                  pl.BlockSpec((tq, LANES), row),
                  pl.BlockSpec((1, S, LANES), per_b), pl.BlockSpec((1, S, KV_W), per_b),
                  pl.BlockSpec((1, S, 2 * KV_W), per_b)],
        out_specs=pl.BlockSpec((tq, N_HEADS * HEAD_DIM), row),
        out_shape=jax.ShapeDtypeStruct((B * S, N_HEADS * HEAD_DIM), BF16),
        scratch_shapes=[pltpu.VMEM((tq, S), F32), pltpu.VMEM((IDX_HEADS, tq, LANES), F32),
                        pltpu.VMEM((tq, LANES), F32), pltpu.VMEM((tq, LANES), F32)]
                       + [pltpu.VMEM((tq, ATT_TK), F32)]
                       + [pltpu.VMEM((N_HEADS, tq, LANES), F32)] * 3
                       + [pltpu.VMEM((N_HEADS, tq, 2 * LANES), F32)]
                       + [pltpu.VMEM((N_HEADS, tq, ATT_TK), F32), pltpu.VMEM((N_HEADS, tq, ATT_TK), BF16)],
        compiler_params=_cparams(("arbitrary", "arbitrary")),
        name="attn_prompt",
    )(q, qi, kiw, kib.reshape(B, S, LANES), kb.reshape(B, S, KV_W), vb.reshape(B, S, 2 * KV_W))


SCORE_PAGES = 32
ATTN_PAGES = 32
SEL_ROWS = 128
SEL_CW = 640


def _sample_scores_kernel(pt_ref, qi_ref, w_ref, kin_ref, *rest):
    del pt_ref
    page_refs, o_ref = rest[:SCORE_PAGES], rest[SCORE_PAGES]
    g = pl.program_id(1)
    t = o_ref.shape[1]
    qi = qi_ref[0]
    wb = _lane_bcast(w_ref[0])

    def page_scores(kpage_t):
        y = jnp.dot(qi, kpage_t.astype(BF16), preferred_element_type=F32)
        z = jnp.maximum(y, 0.0) * wb
        tot = z[0:t]
        for h in range(1, IDX_HEADS):
            tot = tot + z[h * t:(h + 1) * t]
        return tot * IDX_SCALE

    for j in range(SCORE_PAGES):
        col0 = pl.multiple_of((g * SCORE_PAGES + j) * PAGE_SIZE, PAGE_SIZE)
        o_ref[0, :, pl.ds(col0, PAGE_SIZE)] = page_scores(page_refs[j][0])

    @pl.when(g == 0)
    def _():
        past = o_ref.shape[2] - PAGE_SIZE
        qpos = lax.broadcasted_iota(jnp.int32, (t, PAGE_SIZE), 0)
        kpos = lax.broadcasted_iota(jnp.int32, (t, PAGE_SIZE), 1)
        o_ref[0, :, past:] = jnp.where(kpos <= qpos, page_scores(kin_ref[0]), -jnp.inf)


def _sample_scores(page_table, qi_s, w_s, ki_new_t, cache_kidx_t):
    B, n_pages = page_table.shape
    t = qi_s.shape[1] // IDX_HEADS
    past = n_pages * PAGE_SIZE
    per_b = lambda b, g, pt: (b, 0, 0)
    page_specs = [pl.BlockSpec((1, IDX_DIM, PAGE_SIZE),
                               functools.partial(lambda b, g, pt, j: (pt[b, g * SCORE_PAGES + j], 0, 0), j=j))
                  for j in range(SCORE_PAGES)]
    return pl.pallas_call(
        _sample_scores_kernel,
        grid_spec=pltpu.PrefetchScalarGridSpec(
            num_scalar_prefetch=1,
            grid=(B, n_pages // SCORE_PAGES),
            in_specs=[pl.BlockSpec((1, IDX_HEADS * t, IDX_DIM), per_b), pl.BlockSpec((1, IDX_HEADS * t, 1), per_b),
                      pl.BlockSpec((1, IDX_DIM, PAGE_SIZE), per_b)] + page_specs,
            out_specs=pl.BlockSpec((1, t, past + PAGE_SIZE), per_b)),
        out_shape=jax.ShapeDtypeStruct((B, t, past + PAGE_SIZE), F32),
        compiler_params=_cparams(("arbitrary", "arbitrary")),
        name="sample_scores",
    )(page_table, qi_s, w_s, ki_new_t, *([cache_kidx_t] * SCORE_PAGES))


def _select_kernel(sc_ref, tau_ref, jl_ref, *, n_keep):
    tau_ref[...] = _select_rows(sc_ref, sc_ref.shape[1] // SEL_CW, SEL_CW, n_keep, jl_ref)


def _sample_select(scores2d, n_keep):
    r, l = scores2d.shape
    rows = min(SEL_ROWS, r)
    row = lambda i: (i, 0)
    return pl.pallas_call(
        functools.partial(_select_kernel, n_keep=n_keep),
        grid=(r // rows,),
        in_specs=[pl.BlockSpec((rows, l), row)],
        out_specs=[pl.BlockSpec((rows, LANES), row), pl.BlockSpec((rows, LANES), row)],
        out_shape=[jax.ShapeDtypeStruct((r, LANES), F32), jax.ShapeDtypeStruct((r, LANES), F32)],
        compiler_params=_cparams(("arbitrary",)),
        name="sample_select",
    )(scores2d)


def _sample_attn_kernel(pt_ref, q_ref, sc_ref, tau_ref, jl_ref, kn_ref, vn_ref, *rest):
    del pt_ref
    k_refs, v_refs = rest[:ATTN_PAGES], rest[ATTN_PAGES:2 * ATTN_PAGES]
    o_ref, m_ref, l_ref, acc_ref, kbuf_ref, vbuf_ref = rest[2 * ATTN_PAGES:]
    g = pl.program_id(1)
    t = sc_ref.shape[1]
    group = N_HEADS // N_KV_HEADS
    span = ATTN_PAGES * PAGE_SIZE

    @pl.when(g == 0)
    def _():
        m_ref[...] = jnp.full(m_ref.shape, NEG_BIG, F32)
        l_ref[...] = jnp.zeros(l_ref.shape, F32)
        acc_ref[...] = jnp.zeros(acc_ref.shape, F32)

    tau = tau_ref[0][:, 0:1]
    jl = jl_ref[0][:, 0:1]

    def attend(k_of, v_of, x, col0f):
        col = lax.broadcasted_iota(jnp.int32, x.shape, 1).astype(F32) + col0f
        sel = (x > tau) | ((x == tau) & (col <= jl))
        sel = jnp.concatenate([jnp.where(sel, 1.0, 0.0)] * group, axis=0) > 0.5
        for kv in range(N_KV_HEADS):
            rows = slice(kv * group * t, (kv + 1) * group * t)
            s = jnp.dot(q_ref[0, rows, :], k_of(kv), preferred_element_type=F32)
            s = jnp.where(sel, s, NEG_BIG)
            m_old = m_ref[rows]
            m_new = jnp.maximum(m_old, _lane_bcast(jnp.max(s, axis=-1, keepdims=True)))
            alpha = jnp.exp2(m_old - m_new)
            p = jnp.exp2(s - m_new[:, 0:1])
            l_ref[rows] = alpha * l_ref[rows] + _lane_bcast(jnp.sum(p, axis=-1, keepdims=True))
            pv = lax.dot_general(p.astype(BF16), v_of(kv), _NT, preferred_element_type=F32)
            acc_ref[rows] = alpha[:, :HEAD_DIM] * acc_ref[rows] + pv
            m_ref[rows] = m_new

    for j in range(ATTN_PAGES):
        kbuf_ref[:, :, j * PAGE_SIZE:(j + 1) * PAGE_SIZE] = k_refs[j][0].astype(BF16)
        vbuf_ref[:, :, j * PAGE_SIZE:(j + 1) * PAGE_SIZE] = v_refs[j][0].astype(BF16)
    col0 = pl.multiple_of(g * span, span)
    attend(lambda kv: kbuf_ref[kv], lambda kv: vbuf_ref[kv], sc_ref[0, :, pl.ds(col0, span)], col0.astype(F32))

    @pl.when(g == pl.num_programs(1) - 1)
    def _():
        past = sc_ref.shape[2] - PAGE_SIZE
        attend(lambda kv: kn_ref[0, kv], lambda kv: vn_ref[0, kv], sc_ref[0, :, past:], float(past))
        o_ref[0] = acc_ref[...] / l_ref[:, :HEAD_DIM]


def _sample_attn(page_table, q_s, scores, tau, jl, k_new_t, v_new_t, cache_k_t, cache_v_t):
    B, n_pages = page_table.shape
    t = scores.shape[1]
    rows = N_HEADS * t
    per_b3 = lambda b, g, pt: (b, 0, 0)
    per_b4 = lambda b, g, pt: (b, 0, 0, 0)
    page_blk = (1, N_KV_HEADS, HEAD_DIM, PAGE_SIZE)

    def page_specs():
        return [pl.BlockSpec(page_blk,
                             functools.partial(lambda b, g, pt, j: (pt[b, g * ATTN_PAGES + j], 0, 0, 0), j=j))
                for j in range(ATTN_PAGES)]

    return pl.pallas_call(
        _sample_attn_kernel,
        grid_spec=pltpu.PrefetchScalarGridSpec(
            num_scalar_prefetch=1,
            grid=(B, n_pages // ATTN_PAGES),
            in_specs=[pl.BlockSpec((1, rows, HEAD_DIM), per_b3), pl.BlockSpec((1, t, scores.shape[2]), per_b3),
                      pl.BlockSpec((1, t, LANES), per_b3), pl.BlockSpec((1, t, LANES), per_b3),
                      pl.BlockSpec(page_blk, per_b4), pl.BlockSpec(page_blk, per_b4)]
                     + page_specs() + page_specs(),
            out_specs=pl.BlockSpec((1, rows, HEAD_DIM), per_b3),
            scratch_shapes=[pltpu.VMEM((rows, LANES), F32)] * 2 + [pltpu.VMEM((rows, HEAD_DIM), F32)]
                           + [pltpu.VMEM((N_KV_HEADS, HEAD_DIM, ATTN_PAGES * PAGE_SIZE), BF16)] * 2),
        out_shape=jax.ShapeDtypeStruct((B, rows, HEAD_DIM), F32),
        compiler_params=_cparams(("arbitrary", "arbitrary")),
        name="sample_attn",
    )(page_table, q_s, scores, tau, jl, k_new_t, v_new_t,
      *([cache_k_t] * ATTN_PAGES), *([cache_v_t] * ATTN_PAGES))


def _sample_attention(sr, B, T, page_table, cache_k_l, cache_v_l, cache_kidx_l):
    n_pages = page_table.shape[1]
    n_keep = min(TOPK_MAX, (n_pages * PAGE_SIZE + T) // 4)

    def head_major(a, lane_offset):
        a = a.reshape(B, T, N_HEADS, LANES)
        heads = [a[:, :, h, lane_offset(h):lane_offset(h) + HEAD_DIM] for h in range(N_HEADS)]
        return jnp.stack(heads, axis=1).reshape(B, N_HEADS * T, HEAD_DIM)

    def new_transposed(a, heads):
        a = jnp.transpose(a.reshape(B, T, heads, -1), (0, 2, 3, 1))
        return jnp.pad(a, ((0, 0), (0, 0), (0, 0), (0, PAGE_SIZE - T)))

    kidx_t = jnp.transpose(cache_kidx_l, (0, 2, 1))
    k_t = jnp.transpose(cache_k_l, (0, 2, 3, 1))
    v_t = jnp.transpose(cache_v_l, (0, 2, 3, 1))

    qi_s = head_major(sr["qi"], lambda h: 0)
    w_s = jnp.transpose(sr["kiw"][:, IDX_DIM:IDX_DIM + IDX_HEADS].reshape(B, T, IDX_HEADS), (0, 2, 1))
    w_s = w_s.reshape(B, IDX_HEADS * T, 1)
    scores = _sample_scores(page_table, qi_s, w_s, new_transposed(sr["kiw"][:, :IDX_DIM], 1)[:, 0], kidx_t)
    tau, jl = _sample_select(scores.reshape(B * T, -1), n_keep)
    o = _sample_attn(page_table, head_major(sr["q"], _kv_lane_offset), scores, tau.reshape(B, T, LANES),
                     jl.reshape(B, T, LANES), new_transposed(sr["kb"], N_KV_HEADS),
                     new_transposed(sr["v"].astype(BF16), N_KV_HEADS), k_t, v_t)
    o = jnp.transpose(o.reshape(B, N_HEADS, T, HEAD_DIM), (0, 2, 1, 3))
    return o.reshape(B * T, N_HEADS * HEAD_DIM).astype(BF16)


def _layer(x, mods, conv_past, attn_fn, p, tiles):
    B, T, _ = x.shape
    sh1, sc1, g1, sh2, sc2, g2 = mods
    pr = _inproj(x, sc1, sh1, p["norm1_g"], p["w_pack"], p["q_norm_g"], p["k_norm_g"], *tiles["inproj"])
    cin = pr["cin"].reshape(B, T, CONV_CH)
    if conv_past is None:
        yc = _conv_prompt(pr["cin"], B, T, p["conv_w"], p["conv_b"], p["ln_g"], p["ln_b"])
        conv_state = cin[:, T - (CONV_W - 1):]
    else:
        cpad = jnp.concatenate([conv_past, cin], axis=1)
        yc = _conv_sample(cpad, p["conv_w"], p["conv_b"], p["ln_g"], p["ln_b"]).reshape(B * T, CONV_CH)
        conv_state = cpad[:, T:]
    attn = attn_fn(pr)
    x1, h2, gate = _merge(x, g1, sc2, sh2, yc, attn, pr["gates"], p["w_conv_out"], p["w_attn_o"], p["w_out"],
                          p["norm2_g"], p["w_grp"], p["b_grp"], p["w_exp"], p["b_exp"], *tiles["merge"])
    y = _moe(h2, gate, x1, g2, p["w13"], p["w2b"], *tiles["moe"])
    k = pr["k"].reshape(B, T, N_KV_HEADS, HEAD_DIM)
    v = pr["v"].reshape(B, T, N_KV_HEADS, HEAD_DIM)
    ki = pr["kiw"][:, :IDX_DIM].reshape(B, T, IDX_DIM)
    return y, k, v, ki, conv_state


def kernel(x_prompt, x_sample, cache_k, cache_v, cache_kidx, state_conv, page_table, c_prompt, c_sample,
           w_ada, b_ada, norm1_g, w_in, q_norm_g, k_norm_g, conv_w, conv_b, ln_g, ln_b, w_conv_out, w_attn_o,
           w_out, norm2_g, w_grp, b_grp, w_exp, b_exp, w1, w3, w2):
    depth = w_ada.shape[0]
    Bp, S, _ = x_prompt.shape
    Bs, T, _ = x_sample.shape
    xp, xs = x_prompt, x_sample
    outs = [[] for _ in range(8)]
    c_all = jnp.concatenate([c_prompt, c_sample], axis=0)
    c_all = jnp.pad(c_all, ((0, (-c_all.shape[0]) % 16), (0, 0)))
    tiles_p = {"inproj": (1, min(256, S)), "merge": (1, min(512, S)), "moe": (1, min(1024, S))}
    tiles_s = {"inproj": (min(32, Bs), T), "merge": (min(64, Bs), T), "moe": (min(128, Bs), T)}
    for l in range(depth):
        p = dict(norm1_g=norm1_g[l], w_pack=_pack_w_in(w_in[l]), q_norm_g=q_norm_g[l], k_norm_g=k_norm_g[l],
                 conv_w=conv_w[l], conv_b=conv_b[l], ln_g=ln_g[l], ln_b=ln_b[l], w_conv_out=w_conv_out[l],
                 w_attn_o=w_attn_o[l], w_out=w_out[l], norm2_g=norm2_g[l], w_grp=w_grp[l], b_grp=b_grp[l],
                 w_exp=w_exp[l], b_exp=b_exp[l],
                 w13=jnp.concatenate([w1[l], w3[l]], axis=-1).astype(BF16), w2b=w2[l].astype(BF16))
        mod = _ada(c_all, w_ada[l], b_ada[l])
        mods = [m[:, None, :] for m in jnp.split(mod, 6, axis=-1)]
        mods_p = [m[:Bp] for m in mods]
        mods_s = [m[Bp:Bp + Bs] for m in mods]

        attn_p = lambda pr: _attn_prompt(pr["q"], pr["qi"], pr["kiw"], pr["kib"], pr["kb"], pr["vb"], Bp, S)
        xp, k, v, ki, cv = _layer(xp, mods_p, None, attn_p, p, tiles_p)
        for o, a in zip(outs[:4], (k, v, ki, cv)):
            o.append(a)

        attn_s = lambda pr: _sample_attention(pr, Bs, T, page_table, cache_k[l], cache_v[l], cache_kidx[l])
        xs, k, v, ki, cv = _layer(xs, mods_s, state_conv[l], attn_s, p, tiles_s)
        for o, a in zip(outs[4:], (k, v, ki, cv)):
            o.append(a)
    return (xp, xs) + tuple(jnp.stack(o) for o in outs)
```

```python
import functools

import jax
import jax.numpy as jnp
from jax import lax
from jax.experimental import pallas as pl
from jax.experimental.pallas import tpu as pltpu

F32 = jnp.float32
BF16 = jnp.bfloat16

D_MODEL = 1024
CONV_CH = 512
CONV_W = 31
N_HEADS = 8
HEAD_DIM = 64
N_KV_HEADS = 4
KV_W = N_KV_HEADS * HEAD_DIM
IDX_HEADS = 8
IDX_DIM = 64
IDX_SCALE = (IDX_HEADS * IDX_DIM) ** -0.5
TOPK_MAX = 256
PAGE_SIZE = 128
N_GROUPS = 4
EXP_PER_GROUP = 8
N_EXPERTS = 32
EXPERT_FF = 256
EPS = 1e-6

LANES = 128
SUBLANES = 8
LOG2E = 1.4426950408889634
NEG_BIG = -1e30
VMEM_LIMIT = 56 * 1024 * 1024

C_GLU = 0
C_Q = C_GLU + 2 * CONV_CH
C_K = C_Q + N_HEADS * LANES
C_V = C_K + KV_W
C_QI = C_V + KV_W
C_KIW = C_QI + IDX_HEADS * LANES
C_GATES = C_KIW + LANES
IN_PACK = C_GATES + 2 * D_MODEL

N_BISECT = 16


def _cparams(sem):
    return pltpu.CompilerParams(dimension_semantics=sem, vmem_limit_bytes=VMEM_LIMIT)


def _sigmoid(x):
    return 1.0 / (1.0 + jnp.exp(-x))


def _silu(x):
    return x * _sigmoid(x)


def _const_spec(shape):
    nd = len(shape)
    return pl.BlockSpec(shape, lambda *_: (0,) * nd)


def _ada_kernel(c_ref, w_ref, b_ref, o_ref):
    s = _silu(c_ref[...])
    o_ref[...] = jnp.dot(s.astype(BF16), w_ref[...], preferred_element_type=F32) + b_ref[...]


def _ada(c, w_ada, b_ada):
    r = c.shape[0]
    tn = 1536
    return pl.pallas_call(
        _ada_kernel,
        grid=(6 * D_MODEL // tn,),
        in_specs=[pl.BlockSpec((r, D_MODEL), lambda j: (0, 0)),
                  pl.BlockSpec((D_MODEL, tn), lambda j: (0, j)),
                  pl.BlockSpec((1, tn), lambda j: (0, j))],
        out_specs=pl.BlockSpec((r, tn), lambda j: (0, j)),
        out_shape=jax.ShapeDtypeStruct((r, 6 * D_MODEL), F32),
        compiler_params=_cparams(("arbitrary",)),
        name="adaln_mod",
    )(c, w_ada.astype(BF16), b_ada.reshape(1, -1))


def _inproj_kernel(x_ref, sc_ref, sh_ref, g_ref, w_ref, qg_ref, kg_ref, ones_ref,
                   cin_ref, q_ref, k_ref, v_ref, kb_ref, vb_ref, qi_ref, kiw_ref, kib_ref, gates_ref):
    bb, tt, _ = x_ref.shape
    x = x_ref[...]
    ms = jnp.mean(x * x, axis=-1, keepdims=True)
    h = x * lax.rsqrt(ms + EPS) * g_ref[...] * (1.0 + sc_ref[...]) + sh_ref[...]
    hb = h.reshape(bb * tt, D_MODEL).astype(BF16)

    def proj(c0, width):
        return jnp.dot(hb, w_ref[:, c0:c0 + width], preferred_element_type=F32)

    glu = proj(C_GLU, 2 * CONV_CH)
    cin_ref[...] = glu[:, :CONV_CH] * _sigmoid(glu[:, CONV_CH:])

    for hd in range(N_HEADS):
        if hd % 2 == 0:
            q2 = proj(C_Q + hd * LANES, 2 * LANES)
        qh = q2[:, (hd % 2) * LANES:(hd % 2 + 1) * LANES]
        msq = jnp.sum(qh * qh, axis=-1, keepdims=True) * (1.0 / HEAD_DIM)
        qn = qh * lax.rsqrt(msq + EPS) * qg_ref[:, hd * LANES:(hd + 1) * LANES]
        q_ref[:, hd * LANES:(hd + 1) * LANES] = (qn * (HEAD_DIM ** -0.5 * LOG2E)).astype(BF16)

    k = proj(C_K, KV_W)
    k2 = k * k
    k2_hi = k2.astype(BF16)
    k2_lo = (k2 - k2_hi.astype(F32)).astype(BF16)
    ksum = (jnp.dot(k2_hi, ones_ref[...], preferred_element_type=F32)
            + jnp.dot(k2_lo, ones_ref[...], preferred_element_type=F32))
    kn = k * lax.rsqrt(ksum * (1.0 / HEAD_DIM) + EPS) * kg_ref[...]
    k_ref[...] = kn
    kb_ref[...] = kn.astype(BF16)

    v = proj(C_V, KV_W)
    v_ref[...] = v
    ones_blk = jnp.ones((v.shape[0], LANES), BF16)
    for pair in range(N_KV_HEADS // 2):
        vb_ref[:, pair * 2 * LANES:pair * 2 * LANES + LANES] = v[:, pair * LANES:(pair + 1) * LANES].astype(BF16)
        vb_ref[:, pair * 2 * LANES + LANES:(pair + 1) * 2 * LANES] = ones_blk

    qi_ref[...] = proj(C_QI, IDX_HEADS * LANES).astype(BF16)

    kiw = proj(C_KIW, LANES)
    kiw_ref[...] = kiw
    lane = lax.broadcasted_iota(jnp.int32, kiw.shape, 1)
    kib_ref[...] = jnp.where(lane < IDX_DIM, kiw, 0.0).astype(BF16)

    gates_ref[...] = _sigmoid(proj(C_GATES, 2 * D_MODEL))


def _pack_w_in(w_in):
    o = 0
    segs = {}
    for name, width in (("glu", 2 * CONV_CH), ("q", N_HEADS * HEAD_DIM), ("k", KV_W), ("v", KV_W),
                        ("qi", IDX_HEADS * IDX_DIM), ("ki", IDX_DIM), ("wi", IDX_HEADS), ("gates", 2 * D_MODEL)):
        segs[name] = w_in[:, o:o + width]
        o += width
    zeros64 = jnp.zeros((D_MODEL, HEAD_DIM), F32)
    cols = [segs["glu"]]
    for hd in range(N_HEADS):
        wq = segs["q"][:, hd * HEAD_DIM:(hd + 1) * HEAD_DIM]
        cols += [zeros64, wq] if (hd // 2) % 2 else [wq, zeros64]
    cols += [segs["k"], segs["v"]]
    for hd in range(IDX_HEADS):
        cols += [segs["qi"][:, hd * IDX_DIM:(hd + 1) * IDX_DIM], zeros64]
    cols += [segs["ki"], segs["wi"], jnp.zeros((D_MODEL, LANES - IDX_DIM - IDX_HEADS), F32)]
    cols += [segs["gates"]]
    return jnp.concatenate(cols, axis=1).astype(BF16)


def _pack_q_gain(q_norm_g):
    z = jnp.zeros((HEAD_DIM,), F32)
    parts = []
    for hd in range(N_HEADS):
        parts += [z, q_norm_g] if (hd // 2) % 2 else [q_norm_g, z]
    return jnp.concatenate(parts).reshape(1, N_HEADS * LANES)


def _inproj(x, sc1, sh1, norm1_g, w_pack, q_norm_g, k_norm_g, bb, tt):
    B, T, _ = x.shape
    n = B * T
    tm = bb * tt
    nb_t = T // tt
    grid = (B // bb, nb_t)
    row = lambda i, j: (i * nb_t + j, 0)
    ones = (lax.broadcasted_iota(jnp.int32, (KV_W, KV_W), 0) // HEAD_DIM
            == lax.broadcasted_iota(jnp.int32, (KV_W, KV_W), 1) // HEAD_DIM).astype(BF16)
    outs = [("cin", CONV_CH, F32), ("q", N_HEADS * LANES, BF16), ("k", KV_W, F32), ("v", KV_W, F32),
            ("kb", KV_W, BF16), ("vb", 2 * KV_W, BF16), ("qi", IDX_HEADS * LANES, BF16), ("kiw", LANES, F32),
            ("kib", LANES, BF16), ("gates", 2 * D_MODEL, F32)]
    res = pl.pallas_call(
        _inproj_kernel,
        grid=grid,
        in_specs=[pl.BlockSpec((bb, tt, D_MODEL), lambda i, j: (i, j, 0)),
                  pl.BlockSpec((bb, 1, D_MODEL), lambda i, j: (i, 0, 0)),
                  pl.BlockSpec((bb, 1, D_MODEL), lambda i, j: (i, 0, 0)),
                  _const_spec((1, 1, D_MODEL)),
                  _const_spec((D_MODEL, IN_PACK)),
                  _const_spec((1, N_HEADS * LANES)),
                  _const_spec((1, KV_W)),
                  _const_spec((KV_W, KV_W))],
        out_specs=[pl.BlockSpec((tm, w), row) for _, w, _ in outs],
        out_shape=[jax.ShapeDtypeStruct((n, w), dt) for _, w, dt in outs],
        compiler_params=_cparams(("arbitrary", "arbitrary")),
        name="in_proj",
    )(x, sc1, sh1, norm1_g.reshape(1, 1, D_MODEL), w_pack, _pack_q_gain(q_norm_g),
      jnp.tile(k_norm_g, N_KV_HEADS).reshape(1, KV_W), ones)
    return {name: r for (name, _, _), r in zip(outs, res)}


CONV_HALO = 32
CONV_ROWS = 32


def _ln_silu(y, g, b):
    mu = jnp.mean(y, axis=-1, keepdims=True)
    yc = y - mu
    var = jnp.mean(yc * yc, axis=-1, keepdims=True)
    return _silu(yc * lax.rsqrt(var + EPS) * g + b)


def _conv_prompt_kernel(cin_ref, w_ref, b_ref, g_ref, lb_ref, o_ref, win_ref, sh_ref):
    tc = cin_ref.shape[0]
    span = CONV_HALO + tc

    @pl.when(pl.program_id(1) == 0)
    def _():
        win_ref[0:CONV_HALO, :] = jnp.zeros((CONV_HALO, CONV_CH), F32)

    win_ref[CONV_HALO:span, :] = cin_ref[...]
    win_ref[span:, :] = jnp.zeros((SUBLANES, CONV_CH), F32)
    for r in range(SUBLANES):
        sh_ref[r] = win_ref[r:r + span, :]
    off = CONV_HALO - (CONV_W - 1)
    for r0 in range(0, tc, CONV_ROWS):
        acc = jnp.zeros((CONV_ROWS, CONV_CH), F32) + b_ref[...]
        for j in range(CONV_W):
            r = (off + j) % SUBLANES
            a = r0 + off + j - r
            acc = acc + sh_ref[r, a:a + CONV_ROWS, :] * w_ref[j:j + 1, :]
        o_ref[r0:r0 + CONV_ROWS, :] = _ln_silu(acc, g_ref[...], lb_ref[...]).astype(BF16)
    win_ref[0:CONV_HALO, :] = cin_ref[tc - CONV_HALO:, :]


def _conv_prompt(cin, B, S, conv_w, conv_b, ln_g, ln_b):
    tc = 256
    nt = S // tc
    return pl.pallas_call(
        _conv_prompt_kernel,
        grid=(B, nt),
        in_specs=[pl.BlockSpec((tc, CONV_CH), lambda b, j: (b * nt + j, 0)),
                  _const_spec((CONV_W, CONV_CH)), _const_spec((1, CONV_CH)),
                  _const_spec((1, CONV_CH)), _const_spec((1, CONV_CH))],
        out_specs=pl.BlockSpec((tc, CONV_CH), lambda b, j: (b * nt + j, 0)),
        out_shape=jax.ShapeDtypeStruct((B * S, CONV_CH), BF16),
        scratch_shapes=[pltpu.VMEM((CONV_HALO + tc + SUBLANES, CONV_CH), F32),
                        pltpu.VMEM((SUBLANES, CONV_HALO + tc, CONV_CH), F32)],
        compiler_params=_cparams(("arbitrary", "arbitrary")),
        name="conv_prompt",
    )(cin, conv_w, conv_b.reshape(1, -1), ln_g.reshape(1, -1), ln_b.reshape(1, -1))


def _conv_sample_kernel(cpad_ref, w_ref, b_ref, g_ref, lb_ref, o_ref):
    bb, tp, _ = cpad_ref.shape
    t = tp - (CONV_W - 1)
    for b in range(bb):
        acc = jnp.zeros((t, CONV_CH), F32) + b_ref[...]
        for j in range(CONV_W):
            acc = acc + cpad_ref[b, j:j + t, :] * w_ref[j:j + 1, :]
        o_ref[b] = _ln_silu(acc, g_ref[...], lb_ref[...]).astype(BF16)


def _conv_sample(cpad, conv_w, conv_b, ln_g, ln_b):
    B, tp, _ = cpad.shape
    t = tp - (CONV_W - 1)
    bb = 8
    return pl.pallas_call(
        _conv_sample_kernel,
        grid=(B // bb,),
        in_specs=[pl.BlockSpec((bb, tp, CONV_CH), lambda i: (i, 0, 0)),
                  _const_spec((CONV_W, CONV_CH)), _const_spec((1, CONV_CH)),
                  _const_spec((1, CONV_CH)), _const_spec((1, CONV_CH))],
        out_specs=pl.BlockSpec((bb, t, CONV_CH), lambda i: (i, 0, 0)),
        out_shape=jax.ShapeDtypeStruct((B, t, CONV_CH), BF16),
        compiler_params=_cparams(("arbitrary",)),
        name="conv_sample",
    )(cpad, conv_w, conv_b.reshape(1, -1), ln_g.reshape(1, -1), ln_b.reshape(1, -1))


R_EXP = 0
R_GRP = N_EXPERTS


def _lane_min_index(mask, lane):
    return jnp.min(jnp.where(mask, lane, float(LANES)), axis=-1, keepdims=True)


def _router_gate(logits):
    lane = lax.broadcasted_iota(jnp.int32, logits.shape, 1).astype(F32)
    is_grp = (lane >= R_GRP) & (lane < R_GRP + N_GROUPS)
    gl = jnp.where(is_grp, logits, -jnp.inf)
    gmax = jnp.max(gl, axis=-1, keepdims=True)
    gexp = jnp.where(is_grp, jnp.exp(gl - gmax), 0.0)
    p_grp = gexp / jnp.sum(gexp, axis=-1, keepdims=True)
    g_top = jnp.max(p_grp, axis=-1, keepdims=True)
    g_idx = _lane_min_index(is_grp & (p_grp == g_top), lane) - R_GRP
    in_grp = (lane >= g_idx * EXP_PER_GROUP) & (lane < (g_idx + 1) * EXP_PER_GROUP)
    el = jnp.where(in_grp, logits, -jnp.inf)
    emax = jnp.max(el, axis=-1, keepdims=True)
    eexp = jnp.where(in_grp, jnp.exp(el - emax), 0.0)
    p_exp = eexp / jnp.sum(eexp, axis=-1, keepdims=True)
    p1 = jnp.max(jnp.where(in_grp, p_exp, -1.0), axis=-1, keepdims=True)
    i1 = _lane_min_index(in_grp & (p_exp == p1), lane)
    rest = in_grp & (lane != i1)
    p2 = jnp.max(jnp.where(rest, p_exp, -1.0), axis=-1, keepdims=True)
    i2 = _lane_min_index(rest & (p_exp == p2), lane)
    tot = p1 + p2
    return jnp.where(lane == i1, g_top * p1 / tot, 0.0) + jnp.where(lane == i2, g_top * p2 / tot, 0.0)


def _merge_kernel(x_ref, g1_ref, sc2_ref, sh2_ref, yc_ref, at_ref, gates_ref, wco_ref, wao_ref, wout_ref,
                  n2g_ref, wrh_ref, wrl_ref, br_ref, x1_ref, h2_ref, gate_ref):
    bb, tt, _ = x_ref.shape
    y_conv = jnp.dot(yc_ref[...], wco_ref[...], preferred_element_type=F32)
    y_attn = jnp.dot(at_ref[...], wao_ref[...], preferred_element_type=F32)
    m = gates_ref[:, :D_MODEL] * y_conv + gates_ref[:, D_MODEL:] * y_attn
    mo = jnp.dot(m.astype(BF16), wout_ref[...], preferred_element_type=F32)
    x1 = x_ref[...] + g1_ref[...] * mo.reshape(bb, tt, D_MODEL)
    x1_ref[...] = x1
    ms = jnp.mean(x1 * x1, axis=-1, keepdims=True)
    h2 = (x1 * lax.rsqrt(ms + EPS) * n2g_ref[...] * (1.0 + sc2_ref[...]) + sh2_ref[...]).reshape(bb * tt, D_MODEL)
    h2_hi = h2.astype(BF16)
    h2_ref[...] = h2_hi
    h2_lo = (h2 - h2_hi.astype(F32)).astype(BF16)
    logits = (jnp.dot(h2_hi, wrh_ref[...], preferred_element_type=F32)
              + jnp.dot(h2_lo, wrh_ref[...], preferred_element_type=F32)
              + jnp.dot(h2_hi, wrl_ref[...], preferred_element_type=F32)) + br_ref[...]
    gate_ref[...] = _router_gate(logits)


def _merge(x, g1, sc2, sh2, yc, attn, gates, w_conv_out, w_attn_o, w_out, norm2_g, w_grp, b_grp, w_exp, b_exp,
           bb, tt):
    B, T, _ = x.shape
    n = B * T
    tm = bb * tt
    nb_t = T // tt
    row = lambda i, j: (i * nb_t + j, 0)
    mod = lambda i, j: (i, 0, 0)
    pad = jnp.zeros((D_MODEL, LANES - N_EXPERTS - N_GROUPS), F32)
    wr = jnp.concatenate([w_exp, w_grp, pad], axis=1)
    wr_hi = wr.astype(BF16)
    br = jnp.concatenate([b_exp, b_grp, jnp.zeros((LANES - N_EXPERTS - N_GROUPS,), F32)]).reshape(1, LANES)
    return pl.pallas_call(
        _merge_kernel,
        grid=(B // bb, nb_t),
        in_specs=[pl.BlockSpec((bb, tt, D_MODEL), lambda i, j: (i, j, 0)),
                  pl.BlockSpec((bb, 1, D_MODEL), mod), pl.BlockSpec((bb, 1, D_MODEL), mod),
                  pl.BlockSpec((bb, 1, D_MODEL), mod),
                  pl.BlockSpec((tm, CONV_CH), row), pl.BlockSpec((tm, N_HEADS * HEAD_DIM), row),
                  pl.BlockSpec((tm, 2 * D_MODEL), row),
                  _const_spec((CONV_CH, D_MODEL)), _const_spec((N_HEADS * HEAD_DIM, D_MODEL)),
                  _const_spec((D_MODEL, D_MODEL)), _const_spec((1, 1, D_MODEL)),
                  _const_spec((D_MODEL, LANES)), _const_spec((D_MODEL, LANES)), _const_spec((1, LANES))],
        out_specs=[pl.BlockSpec((bb, tt, D_MODEL), lambda i, j: (i, j, 0)),
                   pl.BlockSpec((tm, D_MODEL), row), pl.BlockSpec((tm, LANES), row)],
        out_shape=[jax.ShapeDtypeStruct((B, T, D_MODEL), F32), jax.ShapeDtypeStruct((n, D_MODEL), BF16),
                   jax.ShapeDtypeStruct((n, LANES), F32)],
        compiler_params=_cparams(("arbitrary", "arbitrary")),
        name="merge_router",
    )(x, g1, sc2, sh2, yc, attn, gates, w_conv_out.astype(BF16), w_attn_o.astype(BF16), w_out.astype(BF16),
      norm2_g.reshape(1, 1, D_MODEL), wr_hi, (wr - wr_hi.astype(F32)).astype(BF16), br)


def _moe_kernel(h2_ref, gate_ref, w13_ref, w2_ref, x1_ref, g2_ref, o_ref, acc_ref):
    e = pl.program_id(2)
    bb, tt, _ = x1_ref.shape

    @pl.when(e == 0)
    def _():
        acc_ref[...] = jnp.zeros_like(acc_ref)

    ab = jnp.dot(h2_ref[...], w13_ref[0], preferred_element_type=F32)
    lane = lax.broadcasted_iota(jnp.int32, gate_ref.shape, 1)
    gcol = jnp.sum(jnp.where(lane == e, gate_ref[...], 0.0), axis=-1, keepdims=True)
    hid = _silu(ab[:, :EXPERT_FF]) * ab[:, EXPERT_FF:] * gcol
    acc_ref[...] += jnp.dot(hid.astype(BF16), w2_ref[0], preferred_element_type=F32)

    @pl.when(e == N_EXPERTS - 1)
    def _():
        o_ref[...] = x1_ref[...] + g2_ref[...] * acc_ref[...].reshape(bb, tt, D_MODEL)


def _moe(h2, gate, x1, g2, w13, w2b, bb, tt):
    B, T, _ = x1.shape
    tm = bb * tt
    nb_t = T // tt
    row = lambda i, j, e: (i * nb_t + j, 0)
    return pl.pallas_call(
        _moe_kernel,
        grid=(B // bb, nb_t, N_EXPERTS),
        in_specs=[pl.BlockSpec((tm, D_MODEL), row), pl.BlockSpec((tm, LANES), row),
                  pl.BlockSpec((1, D_MODEL, 2 * EXPERT_FF), lambda i, j, e: (e, 0, 0)),
                  pl.BlockSpec((1, EXPERT_FF, D_MODEL), lambda i, j, e: (e, 0, 0)),
                  pl.BlockSpec((bb, tt, D_MODEL), lambda i, j, e: (i, j, 0)),
                  pl.BlockSpec((bb, 1, D_MODEL), lambda i, j, e: (i, 0, 0))],
        out_specs=pl.BlockSpec((bb, tt, D_MODEL), lambda i, j, e: (i, j, 0)),
        out_shape=jax.ShapeDtypeStruct((B, T, D_MODEL), F32),
        scratch_shapes=[pltpu.VMEM((tm, D_MODEL), F32)],
        compiler_params=_cparams(("arbitrary", "arbitrary", "arbitrary")),
        name="moe_dense",
    )(h2, gate, w13, w2b, x1, g2)


def _lane_bcast(col):
    return jnp.broadcast_to(col, (col.shape[0], LANES))


def _row_any(flag_rows):
    return jnp.max(jnp.where(flag_rows, 1.0, 0.0)) > 0.0


def _select_rows(sc_ref, nch, cw, n_keep, jl_ref):
    R = sc_ref.shape[0]
    ng = cw // LANES
    kf = float(n_keep)
    inf = jnp.full((R, LANES), jnp.inf, F32)
    zero = jnp.zeros((R, LANES), F32)
    lane = lax.broadcasted_iota(jnp.int32, (R, LANES), 1).astype(F32)

    def scan(init, fn):
        def body(c, carry):
            base = pl.multiple_of(c * cw, LANES)
            for j in range(ng):
                blk = sc_ref[:, pl.ds(base + j * LANES, LANES)]
                carry = fn(carry, blk, (base + j * LANES).astype(F32))
            return carry
        return lax.fori_loop(0, nch, body, init)

    rsum = lambda a: _lane_bcast(jnp.sum(a, axis=-1, keepdims=True))
    rmin = lambda a: _lane_bcast(jnp.min(a, axis=-1, keepdims=True))
    rmax = lambda a: _lane_bcast(jnp.max(a, axis=-1, keepdims=True))

    mx, mn = scan((-inf, inf), lambda c, x, _: (jnp.maximum(c[0], x),
                                                 jnp.minimum(c[1], jnp.where(x == -jnp.inf, jnp.inf, x))))
    hi0, lo0 = rmax(mx), rmin(mn)

    def bisect(_, lh):
        lo, hi = lh
        mid = 0.5 * lo + 0.5 * hi
        cnt = rsum(scan(zero, lambda c, x, _: c + jnp.where(x >= mid, 1.0, 0.0)))
        ge = cnt >= kf
        return jnp.where(ge, mid, lo), jnp.where(ge, hi, mid)

    lo, _ = lax.fori_loop(0, N_BISECT, bisect, (lo0, hi0))
    tau0 = rmin(scan(inf, lambda c, x, _: jnp.minimum(c, jnp.where(x >= lo, x, jnp.inf))))

    def refine_pass(tau):
        def fn(c, x, _):
            gt = x > tau
            return (c[0] + jnp.where(gt, 1.0, 0.0), jnp.minimum(c[1], jnp.where(gt, x, jnp.inf)),
                    c[2] + jnp.where(x == tau, 1.0, 0.0))
        cgt, nxt, ceq = scan((zero, inf, zero), fn)
        return rsum(cgt), rmin(nxt), rsum(ceq)

    def refine_body(state):
        tau, _, _, _ = state
        cgt, nxt, ceq = refine_pass(tau)
        up = cgt >= kf
        return jnp.where(up, nxt, tau), cgt, ceq, _row_any(up)

    tau, cgt, ceq, _ = lax.while_loop(lambda s: s[3], refine_body, (tau0, zero, zero, jnp.bool_(True)))

    need = kf - cgt
    excess = ceq > need
    jl_ref[...] = inf

    @pl.when(_row_any(excess))
    def _():
        def idx_bisect(_, lh):
            lo_j, hi_j = lh
            mid = jnp.floor((lo_j + hi_j) * 0.5)
            cnt = rsum(scan(zero, lambda c, x, col0: c + jnp.where((x == tau) & (col0 + lane <= mid), 1.0, 0.0)))
            ge = cnt >= need
            return jnp.where(ge, lo_j, mid), jnp.where(ge, mid, hi_j)

        ncol = (nch * cw).astype(F32) if hasattr(nch, "astype") else float(nch * cw)
        _, hi_j = lax.fori_loop(0, 14, idx_bisect, (zero - 1.0, zero + (ncol - 1.0)))
        jl_ref[...] = jnp.where(excess, hi_j, jnp.inf)

    return tau


ATT_TQ = 128
ATT_TK = 512
IDX_TN = 256
SEL_CHUNK = 1024
BIAS_ROWS = 16

_NT = (((1,), (1,)), ((), ()))


def _kv_lane_offset(head):
    return ((head // 2) % 2) * HEAD_DIM


def _assemble_heads(per_head):
    lane = lax.broadcasted_iota(jnp.int32, per_head[0].shape, 1)
    cols = []
    for c in range(N_HEADS // 2):
        a, b = per_head[2 * c], per_head[2 * c + 1]
        if _kv_lane_offset(2 * c) == 0:
            b = pltpu.roll(b, HEAD_DIM, 1)
        else:
            a = pltpu.roll(a, HEAD_DIM, 1)
        cols.append(jnp.where(lane < HEAD_DIM, a, b))
    return jnp.concatenate(cols, axis=1)


def _attn_prompt_kernel(q_ref, qi_ref, kiw_ref, kib_ref, kb_ref, vb_ref, o_ref,
                        sc_ref, wb_ref, jl_ref, tau_ref, bias_ref, m_ref, al_ref, pm_ref, acc_ref, s_ref, p_ref,
                        *, n_keep):
    tq, tk = ATT_TQ, ATT_TK
    qb = pl.program_id(1)
    t0 = qb * tq
    nkb = (t0 + tq + tk - 1) // tk

    for h in range(IDX_HEADS):
        wb_ref[h] = _lane_bcast(kiw_ref[:, IDX_DIM + h:IDX_DIM + h + 1])

    row_g = t0 + lax.broadcasted_iota(jnp.int32, (tq, IDX_TN), 0)
    col_l = lax.broadcasted_iota(jnp.int32, (tq, IDX_TN), 1)

    def score_body(c, _):
        for j in range(tk // IDX_TN):
            c0 = pl.multiple_of(c * tk + j * IDX_TN, IDX_TN)
            kt = kib_ref[0, pl.ds(c0, IDX_TN), :]
            acc = jnp.zeros((tq, IDX_TN), F32)
            for h in range(IDX_HEADS):
                y = lax.dot_general(qi_ref[:, h * LANES:(h + 1) * LANES], kt, _NT, preferred_element_type=F32)
                wb = wb_ref[h]
                acc = acc + jnp.maximum(y, 0.0) * jnp.concatenate([wb] * (IDX_TN // LANES), axis=1)
            sc_ref[:, pl.ds(c0, IDX_TN)] = jnp.where(c0 + col_l <= row_g, acc * IDX_SCALE, -jnp.inf)
        return 0

    lax.fori_loop(0, nkb, score_body, 0)

    per_chunk = SEL_CHUNK // tk
    for k in range(1, per_chunk):
        @pl.when(nkb % per_chunk == k)
        def _():
            for j in range(k, per_chunk):
                sc_ref[:, pl.ds(pl.multiple_of((nkb - k + j) * tk, tk), tk)] = jnp.full((tq, tk), -jnp.inf, F32)

    tau_ref[...] = _select_rows(sc_ref, (nkb + per_chunk - 1) // per_chunk, SEL_CHUNK, n_keep, jl_ref)

    m_ref[...] = jnp.full(m_ref.shape, NEG_BIG, F32)
    acc_ref[...] = jnp.zeros(acc_ref.shape, F32)
    col_k = lax.broadcasted_iota(jnp.int32, (BIAS_ROWS, tk), 1).astype(F32)
    heads_per_pair = N_HEADS // (N_KV_HEADS // 2)
    ncg = tk // LANES

    def tile_lanes(a):
        return jnp.concatenate([a] * ncg, axis=1)

    def attn_body(c, _):
        c0 = pl.multiple_of(c * tk, tk)
        col = col_k + c0.astype(F32)
        for r0 in range(0, tq, BIAS_ROWS):
            rows = slice(r0, r0 + BIAS_ROWS)
            x = sc_ref[rows, pl.ds(c0, tk)]
            tau = tile_lanes(tau_ref[rows, :])
            sel = (x > tau) | ((x == tau) & (col <= tile_lanes(jl_ref[rows, :])))
            bias_ref[rows, :] = jnp.where(sel, 0.0, NEG_BIG)

        def logits(h):
            pair = h // heads_per_pair
            kt = kb_ref[0, pl.ds(c0, tk), pair * LANES:(pair + 1) * LANES]
            s = lax.dot_general(q_ref[:, h * LANES:(h + 1) * LANES], kt, _NT, preferred_element_type=F32)
            s_ref[h] = s + bias_ref[...]
            pm = s_ref[h, :, 0:LANES]
            for j in range(1, ncg):
                pm = jnp.maximum(pm, s_ref[h, :, j * LANES:(j + 1) * LANES])
            pm_ref[h] = pm

        def new_max(h):
            m_old = m_ref[h]
            m_new = jnp.maximum(m_old, _lane_bcast(jnp.max(pm_ref[h], axis=-1, keepdims=True)))
            al_ref[h] = jnp.exp2(m_old - m_new)
            m_ref[h] = m_new

        def probs(h):
            p_ref[h] = jnp.exp2(s_ref[h] - tile_lanes(m_ref[h])).astype(BF16)

        def accumulate(h):
            pair = h // heads_per_pair
            vt = vb_ref[0, pl.ds(c0, tk), pair * 2 * LANES:(pair + 1) * 2 * LANES]
            alpha = al_ref[h]
            acc_ref[h] = (jnp.concatenate([alpha, alpha], axis=1) * acc_ref[h]
                          + jnp.dot(p_ref[h], vt, preferred_element_type=F32))

        half = N_HEADS // 2
        for h in range(half):
            logits(h)
        for h in range(half):
            new_max(h)
        for h in range(half):
            logits(half + h)
            probs(h)
        for h in range(half):
            new_max(half + h)
        for h in range(half):
            accumulate(h)
            probs(half + h)
        for h in range(half):
            accumulate(half + h)
        return 0

    lax.fori_loop(0, nkb, attn_body, 0)
    o_ref[...] = _assemble_heads([acc_ref[h, :, :LANES] / acc_ref[h, :, LANES:]
                                  for h in range(N_HEADS)]).astype(BF16)


def _attn_prompt(q, qi, kiw, kib, kb, vb, B, S):
    tq = ATT_TQ
    assert S % SEL_CHUNK == 0 and SEL_CHUNK % ATT_TK == 0 and ATT_TK % tq == 0
    nq = S // tq
    n_keep = min(TOPK_MAX, S // 4)
    row = lambda b, j: (b * nq + j, 0)
    per_b = lambda b, j: (b, 0, 0)
    return pl.pallas_call(
        functools.partial(_attn_prompt_kernel, n_keep=n_keep),
        grid=(B, nq),
        in_specs=[pl.BlockSpec((tq, N_HEADS * LANES), row), pl.BlockSpec((tq, IDX_HEADS * LANES), row),
                  pl.BlockSpec((tq, LANES), row),
                  pl.BlockSpec((1, S, LANES), per_b), pl.BlockSpec((1, S, KV_W), per_b),
                  pl.BlockSpec((1, S, 2 * KV_W), per_b)],
        out_specs=pl.BlockSpec((tq, N_HEADS * HEAD_DIM), row),
        out_shape=jax.ShapeDtypeStruct((B * S, N_HEADS * HEAD_DIM), BF16),
        scratch_shapes=[pltpu.VMEM((tq, S), F32), pltpu.VMEM((IDX_HEADS, tq, LANES), F32),
                        pltpu.VMEM((tq, LANES), F32), pltpu.VMEM((tq, LANES), F32)]
                       + [pltpu.VMEM((tq, ATT_TK), F32)]
                       + [pltpu.VMEM((N_HEADS, tq, LANES), F32)] * 3
                       + [pltpu.VMEM((N_HEADS, tq, 2 * LANES), F32)]
                       + [pltpu.VMEM((N_HEADS, tq, ATT_TK), F32), pltpu.VMEM((N_HEADS, tq, ATT_TK), BF16)],
        compiler_params=_cparams(("arbitrary", "arbitrary")),
        name="attn_prompt",
    )(q, qi, kiw, kib.reshape(B, S, LANES), kb.reshape(B, S, KV_W), vb.reshape(B, S, 2 * KV_W))


SCORE_PAGES = 64
ATTN_PAGES = 64
SEL_ROWS = 128
SEL_CW = 640


def _sample_scores_kernel(pt_ref, qi_ref, w_ref, kin_ref, *rest):
    del pt_ref
    page_refs, o_ref = rest[:SCORE_PAGES], rest[SCORE_PAGES]
    g = pl.program_id(1)
    t = o_ref.shape[1]
    qi = qi_ref[0]
    wb = _lane_bcast(w_ref[0])

    def page_scores(kpage_t):
        y = jnp.dot(qi, kpage_t.astype(BF16), preferred_element_type=F32)
        z = jnp.maximum(y, 0.0) * wb
        tot = z[0:t]
        for h in range(1, IDX_HEADS):
            tot = tot + z[h * t:(h + 1) * t]
        return tot * IDX_SCALE

    for j in range(SCORE_PAGES):
        col0 = pl.multiple_of((g * SCORE_PAGES + j) * PAGE_SIZE, PAGE_SIZE)
        o_ref[0, :, pl.ds(col0, PAGE_SIZE)] = page_scores(page_refs[j][0])

    @pl.when(g == 0)
    def _():
        past = o_ref.shape[2] - PAGE_SIZE
        qpos = lax.broadcasted_iota(jnp.int32, (t, PAGE_SIZE), 0)
        kpos = lax.broadcasted_iota(jnp.int32, (t, PAGE_SIZE), 1)
        o_ref[0, :, past:] = jnp.where(kpos <= qpos, page_scores(kin_ref[0]), -jnp.inf)


def _sample_scores(page_table, qi_s, w_s, ki_new_t, cache_kidx_t):
    B, n_pages = page_table.shape
    t = qi_s.shape[1] // IDX_HEADS
    past = n_pages * PAGE_SIZE
    per_b = lambda b, g, pt: (b, 0, 0)
    page_specs = [pl.BlockSpec((1, IDX_DIM, PAGE_SIZE),
                               functools.partial(lambda b, g, pt, j: (pt[b, g * SCORE_PAGES + j], 0, 0), j=j))
                  for j in range(SCORE_PAGES)]
    return pl.pallas_call(
        _sample_scores_kernel,
        grid_spec=pltpu.PrefetchScalarGridSpec(
            num_scalar_prefetch=1,
            grid=(B, n_pages // SCORE_PAGES),
            in_specs=[pl.BlockSpec((1, IDX_HEADS * t, IDX_DIM), per_b), pl.BlockSpec((1, IDX_HEADS * t, 1), per_b),
                      pl.BlockSpec((1, IDX_DIM, PAGE_SIZE), per_b)] + page_specs,
            out_specs=pl.BlockSpec((1, t, past + PAGE_SIZE), per_b)),
        out_shape=jax.ShapeDtypeStruct((B, t, past + PAGE_SIZE), F32),
        compiler_params=_cparams(("arbitrary", "arbitrary")),
        name="sample_scores",
    )(page_table, qi_s, w_s, ki_new_t, *([cache_kidx_t] * SCORE_PAGES))


def _select_kernel(sc_ref, tau_ref, jl_ref, *, n_keep):
    tau_ref[...] = _select_rows(sc_ref, sc_ref.shape[1] // SEL_CW, SEL_CW, n_keep, jl_ref)


def _sample_select(scores2d, n_keep):
    r, l = scores2d.shape
    rows = min(SEL_ROWS, r)
    row = lambda i: (i, 0)
    return pl.pallas_call(
        functools.partial(_select_kernel, n_keep=n_keep),
        grid=(r // rows,),
        in_specs=[pl.BlockSpec((rows, l), row)],
        out_specs=[pl.BlockSpec((rows, LANES), row), pl.BlockSpec((rows, LANES), row)],
        out_shape=[jax.ShapeDtypeStruct((r, LANES), F32), jax.ShapeDtypeStruct((r, LANES), F32)],
        compiler_params=_cparams(("arbitrary",)),
        name="sample_select",
    )(scores2d)


def _sample_attn_kernel(pt_ref, q_ref, sc_ref, tau_ref, jl_ref, kn_ref, vn_ref, *rest):
    del pt_ref
    k_refs, v_refs = rest[:ATTN_PAGES], rest[ATTN_PAGES:2 * ATTN_PAGES]
    o_ref, m_ref, l_ref, acc_ref, kbuf_ref, vbuf_ref = rest[2 * ATTN_PAGES:]
    g = pl.program_id(1)
    t = sc_ref.shape[1]
    group = N_HEADS // N_KV_HEADS
    span = ATTN_PAGES * PAGE_SIZE

    @pl.when(g == 0)
    def _():
        m_ref[...] = jnp.full(m_ref.shape, NEG_BIG, F32)
        l_ref[...] = jnp.zeros(l_ref.shape, F32)
        acc_ref[...] = jnp.zeros(acc_ref.shape, F32)

    tau = tau_ref[0][:, 0:1]
    jl = jl_ref[0][:, 0:1]

    def attend(k_of, v_of, x, col0f):
        col = lax.broadcasted_iota(jnp.int32, x.shape, 1).astype(F32) + col0f
        sel = (x > tau) | ((x == tau) & (col <= jl))
        sel = jnp.concatenate([jnp.where(sel, 1.0, 0.0)] * group, axis=0) > 0.5
        for kv in range(N_KV_HEADS):
            rows = slice(kv * group * t, (kv + 1) * group * t)
            s = jnp.dot(q_ref[0, rows, :], k_of(kv), preferred_element_type=F32)
            s = jnp.where(sel, s, NEG_BIG)
            m_old = m_ref[rows]
            m_new = jnp.maximum(m_old, _lane_bcast(jnp.max(s, axis=-1, keepdims=True)))
            alpha = jnp.exp2(m_old - m_new)
            p = jnp.exp2(s - m_new[:, 0:1])
            l_ref[rows] = alpha * l_ref[rows] + _lane_bcast(jnp.sum(p, axis=-1, keepdims=True))
            pv = lax.dot_general(p.astype(BF16), v_of(kv), _NT, preferred_element_type=F32)
            acc_ref[rows] = alpha[:, :HEAD_DIM] * acc_ref[rows] + pv
            m_ref[rows] = m_new

    for j in range(ATTN_PAGES):
        kbuf_ref[:, :, j * PAGE_SIZE:(j + 1) * PAGE_SIZE] = k_refs[j][0].astype(BF16)
        vbuf_ref[:, :, j * PAGE_SIZE:(j + 1) * PAGE_SIZE] = v_refs[j][0].astype(BF16)
    col0 = pl.multiple_of(g * span, span)
    attend(lambda kv: kbuf_ref[kv], lambda kv: vbuf_ref[kv], sc_ref[0, :, pl.ds(col0, span)], col0.astype(F32))

    @pl.when(g == pl.num_programs(1) - 1)
    def _():
        past = sc_ref.shape[2] - PAGE_SIZE
        attend(lambda kv: kn_ref[0, kv], lambda kv: vn_ref[0, kv], sc_ref[0, :, past:], float(past))
        o_ref[0] = acc_ref[...] / l_ref[:, :HEAD_DIM]


def _sample_attn(page_table, q_s, scores, tau, jl, k_new_t, v_new_t, cache_k_t, cache_v_t):
    B, n_pages = page_table.shape
    t = scores.shape[1]
    rows = N_HEADS * t
    per_b3 = lambda b, g, pt: (b, 0, 0)
    per_b4 = lambda b, g, pt: (b, 0, 0, 0)
    page_blk = (1, N_KV_HEADS, HEAD_DIM, PAGE_SIZE)

    def page_specs():
        return [pl.BlockSpec(page_blk,
                             functools.partial(lambda b, g, pt, j: (pt[b, g * ATTN_PAGES + j], 0, 0, 0), j=j))
                for j in range(ATTN_PAGES)]

    return pl.pallas_call(
        _sample_attn_kernel,
        grid_spec=pltpu.PrefetchScalarGridSpec(
            num_scalar_prefetch=1,
            grid=(B, n_pages // ATTN_PAGES),
            in_specs=[pl.BlockSpec((1, rows, HEAD_DIM), per_b3), pl.BlockSpec((1, t, scores.shape[2]), per_b3),
                      pl.BlockSpec((1, t, LANES), per_b3), pl.BlockSpec((1, t, LANES), per_b3),
                      pl.BlockSpec(page_blk, per_b4), pl.BlockSpec(page_blk, per_b4)]
                     + page_specs() + page_specs(),
            out_specs=pl.BlockSpec((1, rows, HEAD_DIM), per_b3),
            scratch_shapes=[pltpu.VMEM((rows, LANES), F32)] * 2 + [pltpu.VMEM((rows, HEAD_DIM), F32)]
                           + [pltpu.VMEM((N_KV_HEADS, HEAD_DIM, ATTN_PAGES * PAGE_SIZE), BF16)] * 2),
        out_shape=jax.ShapeDtypeStruct((B, rows, HEAD_DIM), F32),
        compiler_params=_cparams(("arbitrary", "arbitrary")),
        name="sample_attn",
    )(page_table, q_s, scores, tau, jl, k_new_t, v_new_t,
      *([cache_k_t] * ATTN_PAGES), *([cache_v_t] * ATTN_PAGES))


def _sample_attention(sr, B, T, page_table, cache_k_l, cache_v_l, cache_kidx_l):
    n_pages = page_table.shape[1]
    n_keep = min(TOPK_MAX, (n_pages * PAGE_SIZE + T) // 4)

    def head_major(a, lane_offset):
        a = a.reshape(B, T, N_HEADS, LANES)
        heads = [a[:, :, h, lane_offset(h):lane_offset(h) + HEAD_DIM] for h in range(N_HEADS)]
        return jnp.stack(heads, axis=1).reshape(B, N_HEADS * T, HEAD_DIM)

    def new_transposed(a, heads):
        a = jnp.transpose(a.reshape(B, T, heads, -1), (0, 2, 3, 1))
        return jnp.pad(a, ((0, 0), (0, 0), (0, 0), (0, PAGE_SIZE - T)))

    kidx_t = jnp.transpose(cache_kidx_l, (0, 2, 1))
    k_t = jnp.transpose(cache_k_l, (0, 2, 3, 1))
    v_t = jnp.transpose(cache_v_l, (0, 2, 3, 1))

    qi_s = head_major(sr["qi"], lambda h: 0)
    w_s = jnp.transpose(sr["kiw"][:, IDX_DIM:IDX_DIM + IDX_HEADS].reshape(B, T, IDX_HEADS), (0, 2, 1))
    w_s = w_s.reshape(B, IDX_HEADS * T, 1)
    scores = _sample_scores(page_table, qi_s, w_s, new_transposed(sr["kiw"][:, :IDX_DIM], 1)[:, 0], kidx_t)
    tau, jl = _sample_select(scores.reshape(B * T, -1), n_keep)
    o = _sample_attn(page_table, head_major(sr["q"], _kv_lane_offset), scores, tau.reshape(B, T, LANES),
                     jl.reshape(B, T, LANES), new_transposed(sr["kb"], N_KV_HEADS),
                     new_transposed(sr["v"].astype(BF16), N_KV_HEADS), k_t, v_t)
    o = jnp.transpose(o.reshape(B, N_HEADS, T, HEAD_DIM), (0, 2, 1, 3))
    return o.reshape(B * T, N_HEADS * HEAD_DIM).astype(BF16)


def _layer(x, mods, conv_past, attn_fn, p, tiles):
    B, T, _ = x.shape
    sh1, sc1, g1, sh2, sc2, g2 = mods
    pr = _inproj(x, sc1, sh1, p["norm1_g"], p["w_pack"], p["q_norm_g"], p["k_norm_g"], *tiles["inproj"])
    cin = pr["cin"].reshape(B, T, CONV_CH)
    if conv_past is None:
        yc = _conv_prompt(pr["cin"], B, T, p["conv_w"], p["conv_b"], p["ln_g"], p["ln_b"])
        conv_state = cin[:, T - (CONV_W - 1):]
    else:
        cpad = jnp.concatenate([conv_past, cin], axis=1)
        yc = _conv_sample(cpad, p["conv_w"], p["conv_b"], p["ln_g"], p["ln_b"]).reshape(B * T, CONV_CH)
        conv_state = cpad[:, T:]
    attn = attn_fn(pr)
    x1, h2, gate = _merge(x, g1, sc2, sh2, yc, attn, pr["gates"], p["w_conv_out"], p["w_attn_o"], p["w_out"],
                          p["norm2_g"], p["w_grp"], p["b_grp"], p["w_exp"], p["b_exp"], *tiles["merge"])
    y = _moe(h2, gate, x1, g2, p["w13"], p["w2b"], *tiles["moe"])
    k = pr["k"].reshape(B, T, N_KV_HEADS, HEAD_DIM)
    v = pr["v"].reshape(B, T, N_KV_HEADS, HEAD_DIM)
    ki = pr["kiw"][:, :IDX_DIM].reshape(B, T, IDX_DIM)
    return y, k, v, ki, conv_state


def kernel(x_prompt, x_sample, cache_k, cache_v, cache_kidx, state_conv, page_table, c_prompt, c_sample,
           w_ada, b_ada, norm1_g, w_in, q_norm_g, k_norm_g, conv_w, conv_b, ln_g, ln_b, w_conv_out, w_attn_o,
           w_out, norm2_g, w_grp, b_grp, w_exp, b_exp, w1, w3, w2):
    depth = w_ada.shape[0]
    Bp, S, _ = x_prompt.shape
    Bs, T, _ = x_sample.shape
    xp, xs = x_prompt, x_sample
    outs = [[] for _ in range(8)]
    c_all = jnp.concatenate([c_prompt, c_sample], axis=0)
    c_all = jnp.pad(c_all, ((0, (-c_all.shape[0]) % 16), (0, 0)))
    tiles_p = {"inproj": (1, min(256, S)), "merge": (1, min(512, S)), "moe": (1, min(1024, S))}
    tiles_s = {"inproj": (min(32, Bs), T), "merge": (min(64, Bs), T), "moe": (min(128, Bs), T)}
    for l in range(depth):
        p = dict(norm1_g=norm1_g[l], w_pack=_pack_w_in(w_in[l]), q_norm_g=q_norm_g[l], k_norm_g=k_norm_g[l],
                 conv_w=conv_w[l], conv_b=conv_b[l], ln_g=ln_g[l], ln_b=ln_b[l], w_conv_out=w_conv_out[l],
                 w_attn_o=w_attn_o[l], w_out=w_out[l], norm2_g=norm2_g[l], w_grp=w_grp[l], b_grp=b_grp[l],
                 w_exp=w_exp[l], b_exp=b_exp[l],
                 w13=jnp.concatenate([w1[l], w3[l]], axis=-1).astype(BF16), w2b=w2[l].astype(BF16))
        mod = _ada(c_all, w_ada[l], b_ada[l])
        mods = [m[:, None, :] for m in jnp.split(mod, 6, axis=-1)]
        mods_p = [m[:Bp] for m in mods]
        mods_s = [m[Bp:Bp + Bs] for m in mods]

        attn_p = lambda pr: _attn_prompt(pr["q"], pr["qi"], pr["kiw"], pr["kib"], pr["kb"], pr["vb"], Bp, S)
        xp, k, v, ki, cv = _layer(xp, mods_p, None, attn_p, p, tiles_p)
        for o, a in zip(outs[:4], (k, v, ki, cv)):
            o.append(a)

        attn_s = lambda pr: _sample_attention(pr, Bs, T, page_table, cache_k[l], cache_v[l], cache_kidx[l])
        xs, k, v, ki, cv = _layer(xs, mods_s, state_conv[l], attn_s, p, tiles_s)
        for o, a in zip(outs[4:], (k, v, ki, cv)):
            o.append(a)
    return (xp, xs) + tuple(jnp.stack(o) for o in outs)
```

```python
import functools

import jax
import jax.numpy as jnp
from jax import lax
from jax.experimental import pallas as pl
from jax.experimental.pallas import tpu as pltpu

F32 = jnp.float32
BF16 = jnp.bfloat16

D_MODEL = 1024
CONV_CH = 512
CONV_W = 31
N_HEADS = 8
HEAD_DIM = 64
N_KV_HEADS = 4
KV_W = N_KV_HEADS * HEAD_DIM
IDX_HEADS = 8
IDX_DIM = 64
IDX_SCALE = (IDX_HEADS * IDX_DIM) ** -0.5
TOPK_MAX = 256
PAGE_SIZE = 128
N_GROUPS = 4
EXP_PER_GROUP = 8
N_EXPERTS = 32
EXPERT_FF = 256
EPS = 1e-6

LANES = 128
SUBLANES = 8
LOG2E = 1.4426950408889634
NEG_BIG = -1e30
VMEM_LIMIT = 56 * 1024 * 1024

C_GLU = 0
C_Q = C_GLU + 2 * CONV_CH
C_K = C_Q + N_HEADS * LANES
C_V = C_K + KV_W
C_QI = C_V + KV_W
C_KIW = C_QI + IDX_HEADS * LANES
C_GATES = C_KIW + LANES
IN_PACK = C_GATES + 2 * D_MODEL

N_BISECT = 16


def _cparams(sem):
    return pltpu.CompilerParams(dimension_semantics=sem, vmem_limit_bytes=VMEM_LIMIT)


def _sigmoid(x):
    return 1.0 / (1.0 + jnp.exp(-x))


def _silu(x):
    return x * _sigmoid(x)


def _const_spec(shape):
    nd = len(shape)
    return pl.BlockSpec(shape, lambda *_: (0,) * nd)


def _ada_kernel(c_ref, w_ref, b_ref, o_ref):
    s = _silu(c_ref[...])
    o_ref[...] = jnp.dot(s.astype(BF16), w_ref[...], preferred_element_type=F32) + b_ref[...]


def _ada(c, w_ada, b_ada):
    r = c.shape[0]
    tn = 1536
    return pl.pallas_call(
        _ada_kernel,
        grid=(6 * D_MODEL // tn,),
        in_specs=[pl.BlockSpec((r, D_MODEL), lambda j: (0, 0)),
                  pl.BlockSpec((D_MODEL, tn), lambda j: (0, j)),
                  pl.BlockSpec((1, tn), lambda j: (0, j))],
        out_specs=pl.BlockSpec((r, tn), lambda j: (0, j)),
        out_shape=jax.ShapeDtypeStruct((r, 6 * D_MODEL), F32),
        compiler_params=_cparams(("arbitrary",)),
        name="adaln_mod",
    )(c, w_ada.astype(BF16), b_ada.reshape(1, -1))


def _inproj_kernel(x_ref, sc_ref, sh_ref, g_ref, w_ref, qg_ref, kg_ref, ones_ref,
                   cin_ref, q_ref, k_ref, v_ref, kb_ref, vb_ref, qi_ref, kiw_ref, kib_ref, gates_ref):
    bb, tt, _ = x_ref.shape
    x = x_ref[...]
    ms = jnp.mean(x * x, axis=-1, keepdims=True)
    h = x * lax.rsqrt(ms + EPS) * g_ref[...] * (1.0 + sc_ref[...]) + sh_ref[...]
    hb = h.reshape(bb * tt, D_MODEL).astype(BF16)

    def proj(c0, width):
        return jnp.dot(hb, w_ref[:, c0:c0 + width], preferred_element_type=F32)

    glu = proj(C_GLU, 2 * CONV_CH)
    cin_ref[...] = glu[:, :CONV_CH] * _sigmoid(glu[:, CONV_CH:])

    for hd in range(N_HEADS):
        if hd % 2 == 0:
            q2 = proj(C_Q + hd * LANES, 2 * LANES)
        qh = q2[:, (hd % 2) * LANES:(hd % 2 + 1) * LANES]
        msq = jnp.sum(qh * qh, axis=-1, keepdims=True) * (1.0 / HEAD_DIM)
        qn = qh * lax.rsqrt(msq + EPS) * qg_ref[:, hd * LANES:(hd + 1) * LANES]
        q_ref[:, hd * LANES:(hd + 1) * LANES] = (qn * (HEAD_DIM ** -0.5 * LOG2E)).astype(BF16)

    k = proj(C_K, KV_W)
    k2 = k * k
    k2_hi = k2.astype(BF16)
    k2_lo = (k2 - k2_hi.astype(F32)).astype(BF16)
    ksum = (jnp.dot(k2_hi, ones_ref[...], preferred_element_type=F32)
            + jnp.dot(k2_lo, ones_ref[...], preferred_element_type=F32))
    kn = k * lax.rsqrt(ksum * (1.0 / HEAD_DIM) + EPS) * kg_ref[...]
    k_ref[...] = kn
    kb_ref[...] = kn.astype(BF16)

    v = proj(C_V, KV_W)
    v_ref[...] = v
    ones_blk = jnp.ones((v.shape[0], LANES), BF16)
    for pair in range(N_KV_HEADS // 2):
        vb_ref[:, pair * 2 * LANES:pair * 2 * LANES + LANES] = v[:, pair * LANES:(pair + 1) * LANES].astype(BF16)
        vb_ref[:, pair * 2 * LANES + LANES:(pair + 1) * 2 * LANES] = ones_blk

    qi_ref[...] = proj(C_QI, IDX_HEADS * LANES).astype(BF16)

    kiw = proj(C_KIW, LANES)
    kiw_ref[...] = kiw
    lane = lax.broadcasted_iota(jnp.int32, kiw.shape, 1)
    kib_ref[...] = jnp.where(lane < IDX_DIM, kiw, 0.0).astype(BF16)

    gates_ref[...] = _sigmoid(proj(C_GATES, 2 * D_MODEL))


def _pack_w_in(w_in):
    o = 0
    segs = {}
    for name, width in (("glu", 2 * CONV_CH), ("q", N_HEADS * HEAD_DIM), ("k", KV_W), ("v", KV_W),
                        ("qi", IDX_HEADS * IDX_DIM), ("ki", IDX_DIM), ("wi", IDX_HEADS), ("gates", 2 * D_MODEL)):
        segs[name] = w_in[:, o:o + width]
        o += width
    zeros64 = jnp.zeros((D_MODEL, HEAD_DIM), F32)
    cols = [segs["glu"]]
    for hd in range(N_HEADS):
        wq = segs["q"][:, hd * HEAD_DIM:(hd + 1) * HEAD_DIM]
        cols += [zeros64, wq] if (hd // 2) % 2 else [wq, zeros64]
    cols += [segs["k"], segs["v"]]
    for hd in range(IDX_HEADS):
        cols += [segs["qi"][:, hd * IDX_DIM:(hd + 1) * IDX_DIM], zeros64]
    cols += [segs["ki"], segs["wi"], jnp.zeros((D_MODEL, LANES - IDX_DIM - IDX_HEADS), F32)]
    cols += [segs["gates"]]
    return jnp.concatenate(cols, axis=1).astype(BF16)


def _pack_q_gain(q_norm_g):
    z = jnp.zeros((HEAD_DIM,), F32)
    parts = []
    for hd in range(N_HEADS):
        parts += [z, q_norm_g] if (hd // 2) % 2 else [q_norm_g, z]
    return jnp.concatenate(parts).reshape(1, N_HEADS * LANES)


def _inproj(x, sc1, sh1, norm1_g, w_pack, q_norm_g, k_norm_g, bb, tt):
    B, T, _ = x.shape
    n = B * T
    tm = bb * tt
    nb_t = T // tt
    grid = (B // bb, nb_t)
    row = lambda i, j: (i * nb_t + j, 0)
    ones = (lax.broadcasted_iota(jnp.int32, (KV_W, KV_W), 0) // HEAD_DIM
            == lax.broadcasted_iota(jnp.int32, (KV_W, KV_W), 1) // HEAD_DIM).astype(BF16)
    outs = [("cin", CONV_CH, F32), ("q", N_HEADS * LANES, BF16), ("k", KV_W, F32), ("v", KV_W, F32),
            ("kb", KV_W, BF16), ("vb", 2 * KV_W, BF16), ("qi", IDX_HEADS * LANES, BF16), ("kiw", LANES, F32),
            ("kib", LANES, BF16), ("gates", 2 * D_MODEL, F32)]
    res = pl.pallas_call(
        _inproj_kernel,
        grid=grid,
        in_specs=[pl.BlockSpec((bb, tt, D_MODEL), lambda i, j: (i, j, 0)),
                  pl.BlockSpec((bb, 1, D_MODEL), lambda i, j: (i, 0, 0)),
                  pl.BlockSpec((bb, 1, D_MODEL), lambda i, j: (i, 0, 0)),
                  _const_spec((1, 1, D_MODEL)),
                  _const_spec((D_MODEL, IN_PACK)),
                  _const_spec((1, N_HEADS * LANES)),
                  _const_spec((1, KV_W)),
                  _const_spec((KV_W, KV_W))],
        out_specs=[pl.BlockSpec((tm, w), row) for _, w, _ in outs],
        out_shape=[jax.ShapeDtypeStruct((n, w), dt) for _, w, dt in outs],
        compiler_params=_cparams(("arbitrary", "arbitrary")),
        name="in_proj",
    )(x, sc1, sh1, norm1_g.reshape(1, 1, D_MODEL), w_pack, _pack_q_gain(q_norm_g),
      jnp.tile(k_norm_g, N_KV_HEADS).reshape(1, KV_W), ones)
    return {name: r for (name, _, _), r in zip(outs, res)}


CONV_HALO = 32
CONV_ROWS = 32


def _ln_silu(y, g, b):
    mu = jnp.mean(y, axis=-1, keepdims=True)
    yc = y - mu
    var = jnp.mean(yc * yc, axis=-1, keepdims=True)
    return _silu(yc * lax.rsqrt(var + EPS) * g + b)


def _conv_prompt_kernel(cin_ref, w_ref, b_ref, g_ref, lb_ref, o_ref, win_ref, sh_ref):
    tc = cin_ref.shape[0]
    span = CONV_HALO + tc

    @pl.when(pl.program_id(1) == 0)
    def _():
        win_ref[0:CONV_HALO, :] = jnp.zeros((CONV_HALO, CONV_CH), F32)

    win_ref[CONV_HALO:span, :] = cin_ref[...]
    win_ref[span:, :] = jnp.zeros((SUBLANES, CONV_CH), F32)
    for r in range(SUBLANES):
        sh_ref[r] = win_ref[r:r + span, :]
    off = CONV_HALO - (CONV_W - 1)
    for r0 in range(0, tc, CONV_ROWS):
        acc = jnp.zeros((CONV_ROWS, CONV_CH), F32) + b_ref[...]
        for j in range(CONV_W):
            r = (off + j) % SUBLANES
            a = r0 + off + j - r
            acc = acc + sh_ref[r, a:a + CONV_ROWS, :] * w_ref[j:j + 1, :]
        o_ref[r0:r0 + CONV_ROWS, :] = _ln_silu(acc, g_ref[...], lb_ref[...]).astype(BF16)
    win_ref[0:CONV_HALO, :] = cin_ref[tc - CONV_HALO:, :]


def _conv_prompt(cin, B, S, conv_w, conv_b, ln_g, ln_b):
    tc = 256
    nt = S // tc
    return pl.pallas_call(
        _conv_prompt_kernel,
        grid=(B, nt),
        in_specs=[pl.BlockSpec((tc, CONV_CH), lambda b, j: (b * nt + j, 0)),
                  _const_spec((CONV_W, CONV_CH)), _const_spec((1, CONV_CH)),
                  _const_spec((1, CONV_CH)), _const_spec((1, CONV_CH))],
        out_specs=pl.BlockSpec((tc, CONV_CH), lambda b, j: (b * nt + j, 0)),
        out_shape=jax.ShapeDtypeStruct((B * S, CONV_CH), BF16),
        scratch_shapes=[pltpu.VMEM((CONV_HALO + tc + SUBLANES, CONV_CH), F32),
                        pltpu.VMEM((SUBLANES, CONV_HALO + tc, CONV_CH), F32)],
        compiler_params=_cparams(("arbitrary", "arbitrary")),
        name="conv_prompt",
    )(cin, conv_w, conv_b.reshape(1, -1), ln_g.reshape(1, -1), ln_b.reshape(1, -1))


def _conv_sample_kernel(cpad_ref, w_ref, b_ref, g_ref, lb_ref, o_ref):
    bb, tp, _ = cpad_ref.shape
    t = tp - (CONV_W - 1)
    for b in range(bb):
        acc = jnp.zeros((t, CONV_CH), F32) + b_ref[...]
        for j in range(CONV_W):
            acc = acc + cpad_ref[b, j:j + t, :] * w_ref[j:j + 1, :]
        o_ref[b] = _ln_silu(acc, g_ref[...], lb_ref[...]).astype(BF16)


def _conv_sample(cpad, conv_w, conv_b, ln_g, ln_b):
    B, tp, _ = cpad.shape
    t = tp - (CONV_W - 1)
    bb = 8
    return pl.pallas_call(
        _conv_sample_kernel,
        grid=(B // bb,),
        in_specs=[pl.BlockSpec((bb, tp, CONV_CH), lambda i: (i, 0, 0)),
                  _const_spec((CONV_W, CONV_CH)), _const_spec((1, CONV_CH)),
                  _const_spec((1, CONV_CH)), _const_spec((1, CONV_CH))],
        out_specs=pl.BlockSpec((bb, t, CONV_CH), lambda i: (i, 0, 0)),
        out_shape=jax.ShapeDtypeStruct((B, t, CONV_CH), BF16),
        compiler_params=_cparams(("arbitrary",)),
        name="conv_sample",
    )(cpad, conv_w, conv_b.reshape(1, -1), ln_g.reshape(1, -1), ln_b.reshape(1, -1))


R_EXP = 0
R_GRP = N_EXPERTS


def _lane_min_index(mask, lane):
    return jnp.min(jnp.where(mask, lane, float(LANES)), axis=-1, keepdims=True)


def _router_gate(logits):
    lane = lax.broadcasted_iota(jnp.int32, logits.shape, 1).astype(F32)
    is_grp = (lane >= R_GRP) & (lane < R_GRP + N_GROUPS)
    gl = jnp.where(is_grp, logits, -jnp.inf)
    gmax = jnp.max(gl, axis=-1, keepdims=True)
    gexp = jnp.where(is_grp, jnp.exp(gl - gmax), 0.0)
    p_grp = gexp / jnp.sum(gexp, axis=-1, keepdims=True)
    g_top = jnp.max(p_grp, axis=-1, keepdims=True)
    g_idx = _lane_min_index(is_grp & (p_grp == g_top), lane) - R_GRP
    in_grp = (lane >= g_idx * EXP_PER_GROUP) & (lane < (g_idx + 1) * EXP_PER_GROUP)
    el = jnp.where(in_grp, logits, -jnp.inf)
    emax = jnp.max(el, axis=-1, keepdims=True)
    eexp = jnp.where(in_grp, jnp.exp(el - emax), 0.0)
    p_exp = eexp / jnp.sum(eexp, axis=-1, keepdims=True)
    p1 = jnp.max(jnp.where(in_grp, p_exp, -1.0), axis=-1, keepdims=True)
    i1 = _lane_min_index(in_grp & (p_exp == p1), lane)
    rest = in_grp & (lane != i1)
    p2 = jnp.max(jnp.where(rest, p_exp, -1.0), axis=-1, keepdims=True)
    i2 = _lane_min_index(rest & (p_exp == p2), lane)
    tot = p1 + p2
    return jnp.where(lane == i1, g_top * p1 / tot, 0.0) + jnp.where(lane == i2, g_top * p2 / tot, 0.0)


def _merge_kernel(x_ref, g1_ref, sc2_ref, sh2_ref, yc_ref, at_ref, gates_ref, wco_ref, wao_ref, wout_ref,
                  n2g_ref, wrh_ref, wrl_ref, br_ref, x1_ref, h2_ref, gate_ref):
    bb, tt, _ = x_ref.shape
    y_conv = jnp.dot(yc_ref[...], wco_ref[...], preferred_element_type=F32)
    y_attn = jnp.dot(at_ref[...], wao_ref[...], preferred_element_type=F32)
    m = gates_ref[:, :D_MODEL] * y_conv + gates_ref[:, D_MODEL:] * y_attn
    mo = jnp.dot(m.astype(BF16), wout_ref[...], preferred_element_type=F32)
    x1 = x_ref[...] + g1_ref[...] * mo.reshape(bb, tt, D_MODEL)
    x1_ref[...] = x1
    ms = jnp.mean(x1 * x1, axis=-1, keepdims=True)
    h2 = (x1 * lax.rsqrt(ms + EPS) * n2g_ref[...] * (1.0 + sc2_ref[...]) + sh2_ref[...]).reshape(bb * tt, D_MODEL)
    h2_hi = h2.astype(BF16)
    h2_ref[...] = h2_hi
    h2_lo = (h2 - h2_hi.astype(F32)).astype(BF16)
    logits = (jnp.dot(h2_hi, wrh_ref[...], preferred_element_type=F32)
              + jnp.dot(h2_lo, wrh_ref[...], preferred_element_type=F32)
              + jnp.dot(h2_hi, wrl_ref[...], preferred_element_type=F32)) + br_ref[...]
    gate_ref[...] = _router_gate(logits)


def _merge(x, g1, sc2, sh2, yc, attn, gates, w_conv_out, w_attn_o, w_out, norm2_g, w_grp, b_grp, w_exp, b_exp,
           bb, tt):
    B, T, _ = x.shape
    n = B * T
    tm = bb * tt
    nb_t = T // tt
    row = lambda i, j: (i * nb_t + j, 0)
    mod = lambda i, j: (i, 0, 0)
    pad = jnp.zeros((D_MODEL, LANES - N_EXPERTS - N_GROUPS), F32)
    wr = jnp.concatenate([w_exp, w_grp, pad], axis=1)
    wr_hi = wr.astype(BF16)
    br = jnp.concatenate([b_exp, b_grp, jnp.zeros((LANES - N_EXPERTS - N_GROUPS,), F32)]).reshape(1, LANES)
    return pl.pallas_call(
        _merge_kernel,
        grid=(B // bb, nb_t),
        in_specs=[pl.BlockSpec((bb, tt, D_MODEL), lambda i, j: (i, j, 0)),
                  pl.BlockSpec((bb, 1, D_MODEL), mod), pl.BlockSpec((bb, 1, D_MODEL), mod),
                  pl.BlockSpec((bb, 1, D_MODEL), mod),
                  pl.BlockSpec((tm, CONV_CH), row), pl.BlockSpec((tm, N_HEADS * HEAD_DIM), row),
                  pl.BlockSpec((tm, 2 * D_MODEL), row),
                  _const_spec((CONV_CH, D_MODEL)), _const_spec((N_HEADS * HEAD_DIM, D_MODEL)),
                  _const_spec((D_MODEL, D_MODEL)), _const_spec((1, 1, D_MODEL)),
                  _const_spec((D_MODEL, LANES)), _const_spec((D_MODEL, LANES)), _const_spec((1, LANES))],
        out_specs=[pl.BlockSpec((bb, tt, D_MODEL), lambda i, j: (i, j, 0)),
                   pl.BlockSpec((tm, D_MODEL), row), pl.BlockSpec((tm, LANES), row)],
        out_shape=[jax.ShapeDtypeStruct((B, T, D_MODEL), F32), jax.ShapeDtypeStruct((n, D_MODEL), BF16),
                   jax.ShapeDtypeStruct((n, LANES), F32)],
        compiler_params=_cparams(("arbitrary", "arbitrary")),
        name="merge_router",
    )(x, g1, sc2, sh2, yc, attn, gates, w_conv_out.astype(BF16), w_attn_o.astype(BF16), w_out.astype(BF16),
      norm2_g.reshape(1, 1, D_MODEL), wr_hi, (wr - wr_hi.astype(F32)).astype(BF16), br)


def _moe_kernel(h2_ref, gate_ref, w13_ref, w2_ref, x1_ref, g2_ref, o_ref, acc_ref):
    e = pl.program_id(2)
    bb, tt, _ = x1_ref.shape

    @pl.when(e == 0)
    def _():
        acc_ref[...] = jnp.zeros_like(acc_ref)

    ab = jnp.dot(h2_ref[...], w13_ref[0], preferred_element_type=F32)
    lane = lax.broadcasted_iota(jnp.int32, gate_ref.shape, 1)
    gcol = jnp.sum(jnp.where(lane == e, gate_ref[...], 0.0), axis=-1, keepdims=True)
    hid = _silu(ab[:, :EXPERT_FF]) * ab[:, EXPERT_FF:] * gcol
    acc_ref[...] += jnp.dot(hid.astype(BF16), w2_ref[0], preferred_element_type=F32)

    @pl.when(e == N_EXPERTS - 1)
    def _():
        o_ref[...] = x1_ref[...] + g2_ref[...] * acc_ref[...].reshape(bb, tt, D_MODEL)


def _moe(h2, gate, x1, g2, w13, w2b, bb, tt):
    B, T, _ = x1.shape
    tm = bb * tt
    nb_t = T // tt
    row = lambda i, j, e: (i * nb_t + j, 0)
    return pl.pallas_call(
        _moe_kernel,
        grid=(B // bb, nb_t, N_EXPERTS),
        in_specs=[pl.BlockSpec((tm, D_MODEL), row), pl.BlockSpec((tm, LANES), row),
                  pl.BlockSpec((1, D_MODEL, 2 * EXPERT_FF), lambda i, j, e: (e, 0, 0)),
                  pl.BlockSpec((1, EXPERT_FF, D_MODEL), lambda i, j, e: (e, 0, 0)),
                  pl.BlockSpec((bb, tt, D_MODEL), lambda i, j, e: (i, j, 0)),
                  pl.BlockSpec((bb, 1, D_MODEL), lambda i, j, e: (i, 0, 0))],
        out_specs=pl.BlockSpec((bb, tt, D_MODEL), lambda i, j, e: (i, j, 0)),
        out_shape=jax.ShapeDtypeStruct((B, T, D_MODEL), F32),
        scratch_shapes=[pltpu.VMEM((tm, D_MODEL), F32)],
        compiler_params=_cparams(("arbitrary", "arbitrary", "arbitrary")),
        name="moe_dense",
    )(h2, gate, w13, w2b, x1, g2)


def _lane_bcast(col):
    return jnp.broadcast_to(col, (col.shape[0], LANES))


def _row_any(flag_rows):
    return jnp.max(jnp.where(flag_rows, 1.0, 0.0)) > 0.0


def _select_rows(sc_ref, nch, cw, n_keep, jl_ref):
    R = sc_ref.shape[0]
    ng = cw // LANES
    kf = float(n_keep)
    inf = jnp.full((R, LANES), jnp.inf, F32)
    zero = jnp.zeros((R, LANES), F32)
    lane = lax.broadcasted_iota(jnp.int32, (R, LANES), 1).astype(F32)

    def scan(init, fn):
        def body(c, carry):
            base = pl.multiple_of(c * cw, LANES)
            for j in range(ng):
                blk = sc_ref[:, pl.ds(base + j * LANES, LANES)]
                carry = fn(carry, blk, (base + j * LANES).astype(F32))
            return carry
        return lax.fori_loop(0, nch, body, init)

    rsum = lambda a: _lane_bcast(jnp.sum(a, axis=-1, keepdims=True))
    rmin = lambda a: _lane_bcast(jnp.min(a, axis=-1, keepdims=True))
    rmax = lambda a: _lane_bcast(jnp.max(a, axis=-1, keepdims=True))

    mx, mn = scan((-inf, inf), lambda c, x, _: (jnp.maximum(c[0], x),
                                                 jnp.minimum(c[1], jnp.where(x == -jnp.inf, jnp.inf, x))))
    hi0, lo0 = rmax(mx), rmin(mn)

    def bisect(_, lh):
        lo, hi = lh
        mid = 0.5 * lo + 0.5 * hi
        cnt = rsum(scan(zero, lambda c, x, _: c + jnp.where(x >= mid, 1.0, 0.0)))
        ge = cnt >= kf
        return jnp.where(ge, mid, lo), jnp.where(ge, hi, mid)

    lo, _ = lax.fori_loop(0, N_BISECT, bisect, (lo0, hi0))
    tau0 = rmin(scan(inf, lambda c, x, _: jnp.minimum(c, jnp.where(x >= lo, x, jnp.inf))))

    def refine_pass(tau):
        def fn(c, x, _):
            gt = x > tau
            return (c[0] + jnp.where(gt, 1.0, 0.0), jnp.minimum(c[1], jnp.where(gt, x, jnp.inf)),
                    c[2] + jnp.where(x == tau, 1.0, 0.0))
        cgt, nxt, ceq = scan((zero, inf, zero), fn)
        return rsum(cgt), rmin(nxt), rsum(ceq)

    def refine_body(state):
        tau, _, _, _ = state
        cgt, nxt, ceq = refine_pass(tau)
        up = cgt >= kf
        return jnp.where(up, nxt, tau), cgt, ceq, _row_any(up)

    tau, cgt, ceq, _ = lax.while_loop(lambda s: s[3], refine_body, (tau0, zero, zero, jnp.bool_(True)))

    need = kf - cgt
    excess = ceq > need
    jl_ref[...] = inf

    @pl.when(_row_any(excess))
    def _():
        def idx_bisect(_, lh):
            lo_j, hi_j = lh
            mid = jnp.floor((lo_j + hi_j) * 0.5)
            cnt = rsum(scan(zero, lambda c, x, col0: c + jnp.where((x == tau) & (col0 + lane <= mid), 1.0, 0.0)))
            ge = cnt >= need
            return jnp.where(ge, lo_j, mid), jnp.where(ge, mid, hi_j)

        ncol = (nch * cw).astype(F32) if hasattr(nch, "astype") else float(nch * cw)
        _, hi_j = lax.fori_loop(0, 14, idx_bisect, (zero - 1.0, zero + (ncol - 1.0)))
        jl_ref[...] = jnp.where(excess, hi_j, jnp.inf)

    return tau


ATT_TQ = 128
ATT_TK = 1024
IDX_TN = 256
SEL_CHUNK = 1024
BIAS_ROWS = 16

_NT = (((1,), (1,)), ((), ()))


def _kv_lane_offset(head):
    return ((head // 2) % 2) * HEAD_DIM


def _assemble_heads(per_head):
    lane = lax.broadcasted_iota(jnp.int32, per_head[0].shape, 1)
    cols = []
    for c in range(N_HEADS // 2):
        a, b = per_head[2 * c], per_head[2 * c + 1]
        if _kv_lane_offset(2 * c) == 0:
            b = pltpu.roll(b, HEAD_DIM, 1)
        else:
            a = pltpu.roll(a, HEAD_DIM, 1)
        cols.append(jnp.where(lane < HEAD_DIM, a, b))
    return jnp.concatenate(cols, axis=1)


def _attn_prompt_kernel(q_ref, qi_ref, kiw_ref, kib_ref, kb_ref, vb_ref, o_ref,
                        sc_ref, wb_ref, jl_ref, tau_ref, bias_ref, m_ref, al_ref, pm_ref, acc_ref, s_ref, p_ref,
                        *, n_keep):
    tq, tk = ATT_TQ, ATT_TK
    qb = pl.program_id(1)
    t0 = qb * tq
    nkb = (t0 + tq + tk - 1) // tk

    for h in range(IDX_HEADS):
        wb_ref[h] = _lane_bcast(kiw_ref[:, IDX_DIM + h:IDX_DIM + h + 1])

    row_g = t0 + lax.broadcasted_iota(jnp.int32, (tq, IDX_TN), 0)
    col_l = lax.broadcasted_iota(jnp.int32, (tq, IDX_TN), 1)

    def score_body(c, _):
        for j in range(tk // IDX_TN):
            c0 = pl.multiple_of(c * tk + j * IDX_TN, IDX_TN)
            kt = kib_ref[0, pl.ds(c0, IDX_TN), :]
            acc = jnp.zeros((tq, IDX_TN), F32)
            for h in range(IDX_HEADS):
                y = lax.dot_general(qi_ref[:, h * LANES:(h + 1) * LANES], kt, _NT, preferred_element_type=F32)
                wb = wb_ref[h]
                acc = acc + jnp.maximum(y, 0.0) * jnp.concatenate([wb] * (IDX_TN // LANES), axis=1)
            sc_ref[:, pl.ds(c0, IDX_TN)] = jnp.where(c0 + col_l <= row_g, acc * IDX_SCALE, -jnp.inf)
        return 0

    lax.fori_loop(0, nkb, score_body, 0)

    per_chunk = SEL_CHUNK // tk
    for k in range(1, per_chunk):
        @pl.when(nkb % per_chunk == k)
        def _():
            for j in range(k, per_chunk):
                sc_ref[:, pl.ds(pl.multiple_of((nkb - k + j) * tk, tk), tk)] = jnp.full((tq, tk), -jnp.inf, F32)

    tau_ref[...] = _select_rows(sc_ref, (nkb + per_chunk - 1) // per_chunk, SEL_CHUNK, n_keep, jl_ref)

    m_ref[...] = jnp.full(m_ref.shape, NEG_BIG, F32)
    acc_ref[...] = jnp.zeros(acc_ref.shape, F32)
    col_k = lax.broadcasted_iota(jnp.int32, (BIAS_ROWS, tk), 1).astype(F32)
    heads_per_pair = N_HEADS // (N_KV_HEADS // 2)
    ncg = tk // LANES

    def tile_lanes(a):
        return jnp.concatenate([a] * ncg, axis=1)

    def attn_body(c, _):
        c0 = pl.multiple_of(c * tk, tk)
        col = col_k + c0.astype(F32)
        for r0 in range(0, tq, BIAS_ROWS):
            rows = slice(r0, r0 + BIAS_ROWS)
            x = sc_ref[rows, pl.ds(c0, tk)]
            tau = tile_lanes(tau_ref[rows, :])
            sel = (x > tau) | ((x == tau) & (col <= tile_lanes(jl_ref[rows, :])))
            bias_ref[rows, :] = jnp.where(sel, 0.0, NEG_BIG)

        def logits(h):
            pair = h // heads_per_pair
            kt = kb_ref[0, pl.ds(c0, tk), pair * LANES:(pair + 1) * LANES]
            s = lax.dot_general(q_ref[:, h * LANES:(h + 1) * LANES], kt, _NT, preferred_element_type=F32)
            s_ref[h] = s + bias_ref[...]
            pm = s_ref[h, :, 0:LANES]
            for j in range(1, ncg):
                pm = jnp.maximum(pm, s_ref[h, :, j * LANES:(j + 1) * LANES])
            pm_ref[h] = pm

        def new_max(h):
            m_old = m_ref[h]
            m_new = jnp.maximum(m_old, _lane_bcast(jnp.max(pm_ref[h], axis=-1, keepdims=True)))
            al_ref[h] = jnp.exp2(m_old - m_new)
            m_ref[h] = m_new

        def probs(h):
            p_ref[h] = jnp.exp2(s_ref[h] - tile_lanes(m_ref[h])).astype(BF16)

        def accumulate(h):
            pair = h // heads_per_pair
            vt = vb_ref[0, pl.ds(c0, tk), pair * 2 * LANES:(pair + 1) * 2 * LANES]
            alpha = al_ref[h]
            acc_ref[h] = (jnp.concatenate([alpha, alpha], axis=1) * acc_ref[h]
                          + jnp.dot(p_ref[h], vt, preferred_element_type=F32))

        half = N_HEADS // 2
        for h in range(half):
            logits(h)
        for h in range(half):
            new_max(h)
        for h in range(half):
            logits(half + h)
            probs(h)
        for h in range(half):
            new_max(half + h)
        for h in range(half):
            accumulate(h)
            probs(half + h)
        for h in range(half):
            accumulate(half + h)
        return 0

    lax.fori_loop(0, nkb, attn_body, 0)
    o_ref[...] = _assemble_heads([acc_ref[h, :, :LANES] / acc_ref[h, :, LANES:]
                                  for h in range(N_HEADS)]).astype(BF16)


def _attn_prompt(q, qi, kiw, kib, kb, vb, B, S):
    tq = ATT_TQ
    assert S % SEL_CHUNK == 0 and SEL_CHUNK % ATT_TK == 0 and ATT_TK % tq == 0
    nq = S // tq
    n_keep = min(TOPK_MAX, S // 4)
    row = lambda b, j: (b * nq + j, 0)
    per_b = lambda b, j: (b, 0, 0)
    return pl.pallas_call(
        functools.partial(_attn_prompt_kernel, n_keep=n_keep),
        grid=(B, nq),
        in_specs=[pl.BlockSpec((tq, N_HEADS * LANES), row), pl.BlockSpec((tq, IDX_HEADS * LANES), row),
                  pl.BlockSpec((tq, LANES), row),
                  pl.BlockSpec((1, S, LANES), per_b), pl.BlockSpec((1, S, KV_W), per_b),
                  pl.BlockSpec((1, S, 2 * KV_W), per_b)],
        out_specs=pl.BlockSpec((tq, N_HEADS * HEAD_DIM), row),
        out_shape=jax.ShapeDtypeStruct((B * S, N_HEADS * HEAD_DIM), BF16),
        scratch_shapes=[pltpu.VMEM((tq, S), F32), pltpu.VMEM((IDX_HEADS, tq, LANES), F32),
                        pltpu.VMEM((tq, LANES), F32), pltpu.VMEM((tq, LANES), F32)]
                       + [pltpu.VMEM((tq, ATT_TK), F32)]
                       + [pltpu.VMEM((N_HEADS, tq, LANES), F32)] * 3
                       + [pltpu.VMEM((N_HEADS, tq, 2 * LANES), F32)]
                       + [pltpu.VMEM((N_HEADS, tq, ATT_TK), F32), pltpu.VMEM((N_HEADS, tq, ATT_TK), BF16)],
        compiler_params=_cparams(("arbitrary", "arbitrary")),
        name="attn_prompt",
    )(q, qi, kiw, kib.reshape(B, S, LANES), kb.reshape(B, S, KV_W), vb.reshape(B, S, 2 * KV_W))


SCORE_PAGES = 64
ATTN_PAGES = 64
SEL_ROWS = 128
SEL_CW = 640


def _sample_scores_kernel(pt_ref, qi_ref, w_ref, kin_ref, *rest):
    del pt_ref
    page_refs, o_ref = rest[:SCORE_PAGES], rest[SCORE_PAGES]
    g = pl.program_id(1)
    t = o_ref.shape[1]
    qi = qi_ref[0]
    wb = _lane_bcast(w_ref[0])

    def page_scores(kpage_t):
        y = jnp.dot(qi, kpage_t.astype(BF16), preferred_element_type=F32)
        z = jnp.maximum(y, 0.0) * wb
        tot = z[0:t]
        for h in range(1, IDX_HEADS):
            tot = tot + z[h * t:(h + 1) * t]
        return tot * IDX_SCALE

    for j in range(SCORE_PAGES):
        col0 = pl.multiple_of((g * SCORE_PAGES + j) * PAGE_SIZE, PAGE_SIZE)
        o_ref[0, :, pl.ds(col0, PAGE_SIZE)] = page_scores(page_refs[j][0])

    @pl.when(g == 0)
    def _():
        past = o_ref.shape[2] - PAGE_SIZE
        qpos = lax.broadcasted_iota(jnp.int32, (t, PAGE_SIZE), 0)
        kpos = lax.broadcasted_iota(jnp.int32, (t, PAGE_SIZE), 1)
        o_ref[0, :, past:] = jnp.where(kpos <= qpos, page_scores(kin_ref[0]), -jnp.inf)


def _sample_scores(page_table, qi_s, w_s, ki_new_t, cache_kidx_t):
    B, n_pages = page_table.shape
    t = qi_s.shape[1] // IDX_HEADS
    past = n_pages * PAGE_SIZE
    per_b = lambda b, g, pt: (b, 0, 0)
    page_specs = [pl.BlockSpec((1, IDX_DIM, PAGE_SIZE),
                               functools.partial(lambda b, g, pt, j: (pt[b, g * SCORE_PAGES + j], 0, 0), j=j))
                  for j in range(SCORE_PAGES)]
    return pl.pallas_call(
        _sample_scores_kernel,
        grid_spec=pltpu.PrefetchScalarGridSpec(
            num_scalar_prefetch=1,
            grid=(B, n_pages // SCORE_PAGES),
            in_specs=[pl.BlockSpec((1, IDX_HEADS * t, IDX_DIM), per_b), pl.BlockSpec((1, IDX_HEADS * t, 1), per_b),
                      pl.BlockSpec((1, IDX_DIM, PAGE_SIZE), per_b)] + page_specs,
            out_specs=pl.BlockSpec((1, t, past + PAGE_SIZE), per_b)),
        out_shape=jax.ShapeDtypeStruct((B, t, past + PAGE_SIZE), F32),
        compiler_params=_cparams(("arbitrary", "arbitrary")),
        name="sample_scores",
    )(page_table, qi_s, w_s, ki_new_t, *([cache_kidx_t] * SCORE_PAGES))


def _select_kernel(sc_ref, tau_ref, jl_ref, *, n_keep):
    tau_ref[...] = _select_rows(sc_ref, sc_ref.shape[1] // SEL_CW, SEL_CW, n_keep, jl_ref)


def _sample_select(scores2d, n_keep):
    r, l = scores2d.shape
    rows = min(SEL_ROWS, r)
    row = lambda i: (i, 0)
    return pl.pallas_call(
        functools.partial(_select_kernel, n_keep=n_keep),
        grid=(r // rows,),
        in_specs=[pl.BlockSpec((rows, l), row)],
        out_specs=[pl.BlockSpec((rows, LANES), row), pl.BlockSpec((rows, LANES), row)],
        out_shape=[jax.ShapeDtypeStruct((r, LANES), F32), jax.ShapeDtypeStruct((r, LANES), F32)],
        compiler_params=_cparams(("arbitrary",)),
        name="sample_select",
    )(scores2d)


def _sample_attn_kernel(pt_ref, q_ref, sc_ref, tau_ref, jl_ref, kn_ref, vn_ref, *rest):
    del pt_ref
    k_refs, v_refs = rest[:ATTN_PAGES], rest[ATTN_PAGES:2 * ATTN_PAGES]
    o_ref, m_ref, l_ref, acc_ref, kbuf_ref, vbuf_ref = rest[2 * ATTN_PAGES:]
    g = pl.program_id(1)
    t = sc_ref.shape[1]
    group = N_HEADS // N_KV_HEADS
    span = ATTN_PAGES * PAGE_SIZE

    @pl.when(g == 0)
    def _():
        m_ref[...] = jnp.full(m_ref.shape, NEG_BIG, F32)
        l_ref[...] = jnp.zeros(l_ref.shape, F32)
        acc_ref[...] = jnp.zeros(acc_ref.shape, F32)

    tau = tau_ref[0][:, 0:1]
    jl = jl_ref[0][:, 0:1]

    def attend(k_of, v_of, x, col0f):
        col = lax.broadcasted_iota(jnp.int32, x.shape, 1).astype(F32) + col0f
        sel = (x > tau) | ((x == tau) & (col <= jl))
        sel = jnp.concatenate([jnp.where(sel, 1.0, 0.0)] * group, axis=0) > 0.5
        for kv in range(N_KV_HEADS):
            rows = slice(kv * group * t, (kv + 1) * group * t)
            s = jnp.dot(q_ref[0, rows, :], k_of(kv), preferred_element_type=F32)
            s = jnp.where(sel, s, NEG_BIG)
            m_old = m_ref[rows]
            m_new = jnp.maximum(m_old, _lane_bcast(jnp.max(s, axis=-1, keepdims=True)))
            alpha = jnp.exp2(m_old - m_new)
            p = jnp.exp2(s - m_new[:, 0:1])
            l_ref[rows] = alpha * l_ref[rows] + _lane_bcast(jnp.sum(p, axis=-1, keepdims=True))
            pv = lax.dot_general(p.astype(BF16), v_of(kv), _NT, preferred_element_type=F32)
            acc_ref[rows] = alpha[:, :HEAD_DIM] * acc_ref[rows] + pv
            m_ref[rows] = m_new

    for j in range(ATTN_PAGES):
        kbuf_ref[:, :, j * PAGE_SIZE:(j + 1) * PAGE_SIZE] = k_refs[j][0].astype(BF16)
        vbuf_ref[:, :, j * PAGE_SIZE:(j + 1) * PAGE_SIZE] = v_refs[j][0].astype(BF16)
    col0 = pl.multiple_of(g * span, span)
    attend(lambda kv: kbuf_ref[kv], lambda kv: vbuf_ref[kv], sc_ref[0, :, pl.ds(col0, span)], col0.astype(F32))

    @pl.when(g == pl.num_programs(1) - 1)
    def _():
        past = sc_ref.shape[2] - PAGE_SIZE
        attend(lambda kv: kn_ref[0, kv], lambda kv: vn_ref[0, kv], sc_ref[0, :, past:], float(past))
        o_ref[0] = acc_ref[...] / l_ref[:, :HEAD_DIM]


def _sample_attn(page_table, q_s, scores, tau, jl, k_new_t, v_new_t, cache_k_t, cache_v_t):
    B, n_pages = page_table.shape
    t = scores.shape[1]
    rows = N_HEADS * t
    per_b3 = lambda b, g, pt: (b, 0, 0)
    per_b4 = lambda b, g, pt: (b, 0, 0, 0)
    page_blk = (1, N_KV_HEADS, HEAD_DIM, PAGE_SIZE)

    def page_specs():
        return [pl.BlockSpec(page_blk,
                             functools.partial(lambda b, g, pt, j: (pt[b, g * ATTN_PAGES + j], 0, 0, 0), j=j))
                for j in range(ATTN_PAGES)]

    return pl.pallas_call(
        _sample_attn_kernel,
        grid_spec=pltpu.PrefetchScalarGridSpec(
            num_scalar_prefetch=1,
            grid=(B, n_pages // ATTN_PAGES),
            in_specs=[pl.BlockSpec((1, rows, HEAD_DIM), per_b3), pl.BlockSpec((1, t, scores.shape[2]), per_b3),
                      pl.BlockSpec((1, t, LANES), per_b3), pl.BlockSpec((1, t, LANES), per_b3),
                      pl.BlockSpec(page_blk, per_b4), pl.BlockSpec(page_blk, per_b4)]
                     + page_specs() + page_specs(),
            out_specs=pl.BlockSpec((1, rows, HEAD_DIM), per_b3),
            scratch_shapes=[pltpu.VMEM((rows, LANES), F32)] * 2 + [pltpu.VMEM((rows, HEAD_DIM), F32)]
                           + [pltpu.VMEM((N_KV_HEADS, HEAD_DIM, ATTN_PAGES * PAGE_SIZE), BF16)] * 2),
        out_shape=jax.ShapeDtypeStruct((B, rows, HEAD_DIM), F32),
        compiler_params=_cparams(("arbitrary", "arbitrary")),
        name="sample_attn",
    )(page_table, q_s, scores, tau, jl, k_new_t, v_new_t,
      *([cache_k_t] * ATTN_PAGES), *([cache_v_t] * ATTN_PAGES))


def _sample_attention(sr, B, T, page_table, cache_k_l, cache_v_l, cache_kidx_l):
    n_pages = page_table.shape[1]
    n_keep = min(TOPK_MAX, (n_pages * PAGE_SIZE + T) // 4)

    def head_major(a, lane_offset):
        a = a.reshape(B, T, N_HEADS, LANES)
        heads = [a[:, :, h, lane_offset(h):lane_offset(h) + HEAD_DIM] for h in range(N_HEADS)]
        return jnp.stack(heads, axis=1).reshape(B, N_HEADS * T, HEAD_DIM)

    def new_transposed(a, heads):
        a = jnp.transpose(a.reshape(B, T, heads, -1), (0, 2, 3, 1))
        return jnp.pad(a, ((0, 0), (0, 0), (0, 0), (0, PAGE_SIZE - T)))

    kidx_t = jnp.transpose(cache_kidx_l, (0, 2, 1))
    k_t = jnp.transpose(cache_k_l, (0, 2, 3, 1))
    v_t = jnp.transpose(cache_v_l, (0, 2, 3, 1))

    qi_s = head_major(sr["qi"], lambda h: 0)
    w_s = jnp.transpose(sr["kiw"][:, IDX_DIM:IDX_DIM + IDX_HEADS].reshape(B, T, IDX_HEADS), (0, 2, 1))
    w_s = w_s.reshape(B, IDX_HEADS * T, 1)
    scores = _sample_scores(page_table, qi_s, w_s, new_transposed(sr["kiw"][:, :IDX_DIM], 1)[:, 0], kidx_t)
    tau, jl = _sample_select(scores.reshape(B * T, -1), n_keep)
    o = _sample_attn(page_table, head_major(sr["q"], _kv_lane_offset), scores, tau.reshape(B, T, LANES),
                     jl.reshape(B, T, LANES), new_transposed(sr["kb"], N_KV_HEADS),
                     new_transposed(sr["v"].astype(BF16), N_KV_HEADS), k_t, v_t)
    o = jnp.transpose(o.reshape(B, N_HEADS, T, HEAD_DIM), (0, 2, 1, 3))
    return o.reshape(B * T, N_HEADS * HEAD_DIM).astype(BF16)


def _layer(x, mods, conv_past, attn_fn, p, tiles):
    B, T, _ = x.shape
    sh1, sc1, g1, sh2, sc2, g2 = mods
    pr = _inproj(x, sc1, sh1, p["norm1_g"], p["w_pack"], p["q_norm_g"], p["k_norm_g"], *tiles["inproj"])
    cin = pr["cin"].reshape(B, T, CONV_CH)
    if conv_past is None:
        yc = _conv_prompt(pr["cin"], B, T, p["conv_w"], p["conv_b"], p["ln_g"], p["ln_b"])
        conv_state = cin[:, T - (CONV_W - 1):]
    else:
        cpad = jnp.concatenate([conv_past, cin], axis=1)
        yc = _conv_sample(cpad, p["conv_w"], p["conv_b"], p["ln_g"], p["ln_b"]).reshape(B * T, CONV_CH)
        conv_state = cpad[:, T:]
    attn = attn_fn(pr)
    x1, h2, gate = _merge(x, g1, sc2, sh2, yc, attn, pr["gates"], p["w_conv_out"], p["w_attn_o"], p["w_out"],
                          p["norm2_g"], p["w_grp"], p["b_grp"], p["w_exp"], p["b_exp"], *tiles["merge"])
    y = _moe(h2, gate, x1, g2, p["w13"], p["w2b"], *tiles["moe"])
    k = pr["k"].reshape(B, T, N_KV_HEADS, HEAD_DIM)
    v = pr["v"].reshape(B, T, N_KV_HEADS, HEAD_DIM)
    ki = pr["kiw"][:, :IDX_DIM].reshape(B, T, IDX_DIM)
    return y, k, v, ki, conv_state


def kernel(x_prompt, x_sample, cache_k, cache_v, cache_kidx, state_conv, page_table, c_prompt, c_sample,
           w_ada, b_ada, norm1_g, w_in, q_norm_g, k_norm_g, conv_w, conv_b, ln_g, ln_b, w_conv_out, w_attn_o,
           w_out, norm2_g, w_grp, b_grp, w_exp, b_exp, w1, w3, w2):
    depth = w_ada.shape[0]
    Bp, S, _ = x_prompt.shape
    Bs, T, _ = x_sample.shape
    xp, xs = x_prompt, x_sample
    outs = [[] for _ in range(8)]
    c_all = jnp.concatenate([c_prompt, c_sample], axis=0)
    c_all = jnp.pad(c_all, ((0, (-c_all.shape[0]) % 16), (0, 0)))
    tiles_p = {"inproj": (1, min(256, S)), "merge": (1, min(512, S)), "moe": (1, min(1024, S))}
    tiles_s = {"inproj": (min(32, Bs), T), "merge": (min(64, Bs), T), "moe": (min(128, Bs), T)}
    for l in range(depth):
        p = dict(norm1_g=norm1_g[l], w_pack=_pack_w_in(w_in[l]), q_norm_g=q_norm_g[l], k_norm_g=k_norm_g[l],
                 conv_w=conv_w[l], conv_b=conv_b[l], ln_g=ln_g[l], ln_b=ln_b[l], w_conv_out=w_conv_out[l],
                 w_attn_o=w_attn_o[l], w_out=w_out[l], norm2_g=norm2_g[l], w_grp=w_grp[l], b_grp=b_grp[l],
                 w_exp=w_exp[l], b_exp=b_exp[l],
                 w13=jnp.concatenate([w1[l], w3[l]], axis=-1).astype(BF16), w2b=w2[l].astype(BF16))
        mod = _ada(c_all, w_ada[l], b_ada[l])
        mods = [m[:, None, :] for m in jnp.split(mod, 6, axis=-1)]
        mods_p = [m[:Bp] for m in mods]
        mods_s = [m[Bp:Bp + Bs] for m in mods]

        attn_p = lambda pr: _attn_prompt(pr["q"], pr["qi"], pr["kiw"], pr["kib"], pr["kb"], pr["vb"], Bp, S)
        xp, k, v, ki, cv = _layer(xp, mods_p, None, attn_p, p, tiles_p)
        for o, a in zip(outs[:4], (k, v, ki, cv)):
            o.append(a)

        attn_s = lambda pr: _sample_attention(pr, Bs, T, page_table, cache_k[l], cache_v[l], cache_kidx[l])
        xs, k, v, ki, cv = _layer(xs, mods_s, state_conv[l], attn_s, p, tiles_s)
        for o, a in zip(outs[4:], (k, v, ki, cv)):
            o.append(a)
    return (xp, xs) + tuple(jnp.stack(o) for o in outs)
```

```python
import functools

import jax
import jax.numpy as jnp
from jax import lax
from jax.experimental import pallas as pl
from jax.experimental.pallas import tpu as pltpu

F32 = jnp.float32
BF16 = jnp.bfloat16

D_MODEL = 1024
CONV_CH = 512
CONV_W = 31
N_HEADS = 8
HEAD_DIM = 64
N_KV_HEADS = 4
KV_W = N_KV_HEADS * HEAD_DIM
IDX_HEADS = 8
IDX_DIM = 64
IDX_SCALE = (IDX_HEADS * IDX_DIM) ** -0.5
TOPK_MAX = 256
PAGE_SIZE = 128
N_GROUPS = 4
EXP_PER_GROUP = 8
N_EXPERTS = 32
EXPERT_FF = 256
EPS = 1e-6

LANES = 128
SUBLANES = 8
LOG2E = 1.4426950408889634
NEG_BIG = -1e30
VMEM_LIMIT = 56 * 1024 * 1024

C_GLU = 0
C_Q = C_GLU + 2 * CONV_CH
C_K = C_Q + N_HEADS * LANES
C_V = C_K + KV_W
C_QI = C_V + KV_W
C_KIW = C_QI + IDX_HEADS * LANES
C_GATES = C_KIW + LANES
IN_PACK = C_GATES + 2 * D_MODEL

N_BISECT = 16


def _cparams(sem):
    return pltpu.CompilerParams(dimension_semantics=sem, vmem_limit_bytes=VMEM_LIMIT)


def _sigmoid(x):
    return 1.0 / (1.0 + jnp.exp(-x))


def _silu(x):
    return x * _sigmoid(x)


def _const_spec(shape):
    nd = len(shape)
    return pl.BlockSpec(shape, lambda *_: (0,) * nd)


def _ada_kernel(c_ref, w_ref, b_ref, o_ref):
    s = _silu(c_ref[...])
    o_ref[...] = jnp.dot(s.astype(BF16), w_ref[...], preferred_element_type=F32) + b_ref[...]


def _ada(c, w_ada, b_ada):
    r = c.shape[0]
    tn = 1536
    return pl.pallas_call(
        _ada_kernel,
        grid=(6 * D_MODEL // tn,),
        in_specs=[pl.BlockSpec((r, D_MODEL), lambda j: (0, 0)),
                  pl.BlockSpec((D_MODEL, tn), lambda j: (0, j)),
                  pl.BlockSpec((1, tn), lambda j: (0, j))],
        out_specs=pl.BlockSpec((r, tn), lambda j: (0, j)),
        out_shape=jax.ShapeDtypeStruct((r, 6 * D_MODEL), F32),
        compiler_params=_cparams(("arbitrary",)),
        name="adaln_mod",
    )(c, w_ada.astype(BF16), b_ada.reshape(1, -1))


def _inproj_kernel(x_ref, sc_ref, sh_ref, g_ref, w_ref, qg_ref, kg_ref, ones_ref,
                   cin_ref, q_ref, k_ref, v_ref, kb_ref, vb_ref, qi_ref, kiw_ref, kib_ref, gates_ref):
    bb, tt, _ = x_ref.shape
    x = x_ref[...]
    ms = jnp.mean(x * x, axis=-1, keepdims=True)
    h = x * lax.rsqrt(ms + EPS) * g_ref[...] * (1.0 + sc_ref[...]) + sh_ref[...]
    hb = h.reshape(bb * tt, D_MODEL).astype(BF16)

    def proj(c0, width):
        return jnp.dot(hb, w_ref[:, c0:c0 + width], preferred_element_type=F32)

    glu = proj(C_GLU, 2 * CONV_CH)
    cin_ref[...] = glu[:, :CONV_CH] * _sigmoid(glu[:, CONV_CH:])

    for hd in range(N_HEADS):
        if hd % 2 == 0:
            q2 = proj(C_Q + hd * LANES, 2 * LANES)
        qh = q2[:, (hd % 2) * LANES:(hd % 2 + 1) * LANES]
        msq = jnp.sum(qh * qh, axis=-1, keepdims=True) * (1.0 / HEAD_DIM)
        qn = qh * lax.rsqrt(msq + EPS) * qg_ref[:, hd * LANES:(hd + 1) * LANES]
        q_ref[:, hd * LANES:(hd + 1) * LANES] = (qn * (HEAD_DIM ** -0.5 * LOG2E)).astype(BF16)

    k = proj(C_K, KV_W)
    k2 = k * k
    k2_hi = k2.astype(BF16)
    k2_lo = (k2 - k2_hi.astype(F32)).astype(BF16)
    ksum = (jnp.dot(k2_hi, ones_ref[...], preferred_element_type=F32)
            + jnp.dot(k2_lo, ones_ref[...], preferred_element_type=F32))
    kn = k * lax.rsqrt(ksum * (1.0 / HEAD_DIM) + EPS) * kg_ref[...]
    k_ref[...] = kn
    kb_ref[...] = kn.astype(BF16)

    v = proj(C_V, KV_W)
    v_ref[...] = v
    ones_blk = jnp.ones((v.shape[0], LANES), BF16)
    for pair in range(N_KV_HEADS // 2):
        vb_ref[:, pair * 2 * LANES:pair * 2 * LANES + LANES] = v[:, pair * LANES:(pair + 1) * LANES].astype(BF16)
        vb_ref[:, pair * 2 * LANES + LANES:(pair + 1) * 2 * LANES] = ones_blk

    qi_ref[...] = proj(C_QI, IDX_HEADS * LANES).astype(BF16)

    kiw = proj(C_KIW, LANES)
    kiw_ref[...] = kiw
    lane = lax.broadcasted_iota(jnp.int32, kiw.shape, 1)
    kib_ref[...] = jnp.where(lane < IDX_DIM, kiw, 0.0).astype(BF16)

    gates_ref[...] = _sigmoid(proj(C_GATES, 2 * D_MODEL))


def _pack_w_in(w_in):
    o = 0
    segs = {}
    for name, width in (("glu", 2 * CONV_CH), ("q", N_HEADS * HEAD_DIM), ("k", KV_W), ("v", KV_W),
                        ("qi", IDX_HEADS * IDX_DIM), ("ki", IDX_DIM), ("wi", IDX_HEADS), ("gates", 2 * D_MODEL)):
        segs[name] = w_in[:, o:o + width]
        o += width
    zeros64 = jnp.zeros((D_MODEL, HEAD_DIM), F32)
    cols = [segs["glu"]]
    for hd in range(N_HEADS):
        wq = segs["q"][:, hd * HEAD_DIM:(hd + 1) * HEAD_DIM]
        cols += [zeros64, wq] if (hd // 2) % 2 else [wq, zeros64]
    cols += [segs["k"], segs["v"]]
    for hd in range(IDX_HEADS):
        cols += [segs["qi"][:, hd * IDX_DIM:(hd + 1) * IDX_DIM], zeros64]
    cols += [segs["ki"], segs["wi"], jnp.zeros((D_MODEL, LANES - IDX_DIM - IDX_HEADS), F32)]
    cols += [segs["gates"]]
    return jnp.concatenate(cols, axis=1).astype(BF16)


def _pack_q_gain(q_norm_g):
    z = jnp.zeros((HEAD_DIM,), F32)
    parts = []
    for hd in range(N_HEADS):
        parts += [z, q_norm_g] if (hd // 2) % 2 else [q_norm_g, z]
    return jnp.concatenate(parts).reshape(1, N_HEADS * LANES)


def _inproj(x, sc1, sh1, norm1_g, w_pack, q_norm_g, k_norm_g, bb, tt):
    B, T, _ = x.shape
    n = B * T
    tm = bb * tt
    nb_t = T // tt
    grid = (B // bb, nb_t)
    row = lambda i, j: (i * nb_t + j, 0)
    ones = (lax.broadcasted_iota(jnp.int32, (KV_W, KV_W), 0) // HEAD_DIM
            == lax.broadcasted_iota(jnp.int32, (KV_W, KV_W), 1) // HEAD_DIM).astype(BF16)
    outs = [("cin", CONV_CH, F32), ("q", N_HEADS * LANES, BF16), ("k", KV_W, F32), ("v", KV_W, F32),
            ("kb", KV_W, BF16), ("vb", 2 * KV_W, BF16), ("qi", IDX_HEADS * LANES, BF16), ("kiw", LANES, F32),
            ("kib", LANES, BF16), ("gates", 2 * D_MODEL, F32)]
    res = pl.pallas_call(
        _inproj_kernel,
        grid=grid,
        in_specs=[pl.BlockSpec((bb, tt, D_MODEL), lambda i, j: (i, j, 0)),
                  pl.BlockSpec((bb, 1, D_MODEL), lambda i, j: (i, 0, 0)),
                  pl.BlockSpec((bb, 1, D_MODEL), lambda i, j: (i, 0, 0)),
                  _const_spec((1, 1, D_MODEL)),
                  _const_spec((D_MODEL, IN_PACK)),
                  _const_spec((1, N_HEADS * LANES)),
                  _const_spec((1, KV_W)),
                  _const_spec((KV_W, KV_W))],
        out_specs=[pl.BlockSpec((tm, w), row) for _, w, _ in outs],
        out_shape=[jax.ShapeDtypeStruct((n, w), dt) for _, w, dt in outs],
        compiler_params=_cparams(("arbitrary", "arbitrary")),
        name="in_proj",
    )(x, sc1, sh1, norm1_g.reshape(1, 1, D_MODEL), w_pack, _pack_q_gain(q_norm_g),
      jnp.tile(k_norm_g, N_KV_HEADS).reshape(1, KV_W), ones)
    return {name: r for (name, _, _), r in zip(outs, res)}


CONV_HALO = 32
CONV_ROWS = 32


def _ln_silu(y, g, b):
    mu = jnp.mean(y, axis=-1, keepdims=True)
    yc = y - mu
    var = jnp.mean(yc * yc, axis=-1, keepdims=True)
    return _silu(yc * lax.rsqrt(var + EPS) * g + b)


def _conv_prompt_kernel(cin_ref, w_ref, b_ref, g_ref, lb_ref, o_ref, win_ref, sh_ref):
    tc = cin_ref.shape[0]
    span = CONV_HALO + tc

    @pl.when(pl.program_id(1) == 0)
    def _():
        win_ref[0:CONV_HALO, :] = jnp.zeros((CONV_HALO, CONV_CH), F32)

    win_ref[CONV_HALO:span, :] = cin_ref[...]
    win_ref[span:, :] = jnp.zeros((SUBLANES, CONV_CH), F32)
    for r in range(SUBLANES):
        sh_ref[r] = win_ref[r:r + span, :]
    off = CONV_HALO - (CONV_W - 1)
    for r0 in range(0, tc, CONV_ROWS):
        acc = jnp.zeros((CONV_ROWS, CONV_CH), F32) + b_ref[...]
        for j in range(CONV_W):
            r = (off + j) % SUBLANES
            a = r0 + off + j - r
            acc = acc + sh_ref[r, a:a + CONV_ROWS, :] * w_ref[j:j + 1, :]
        o_ref[r0:r0 + CONV_ROWS, :] = _ln_silu(acc, g_ref[...], lb_ref[...]).astype(BF16)
    win_ref[0:CONV_HALO, :] = cin_ref[tc - CONV_HALO:, :]


def _conv_prompt(cin, B, S, conv_w, conv_b, ln_g, ln_b):
    tc = 256
    nt = S // tc
    return pl.pallas_call(
        _conv_prompt_kernel,
        grid=(B, nt),
        in_specs=[pl.BlockSpec((tc, CONV_CH), lambda b, j: (b * nt + j, 0)),
                  _const_spec((CONV_W, CONV_CH)), _const_spec((1, CONV_CH)),
                  _const_spec((1, CONV_CH)), _const_spec((1, CONV_CH))],
        out_specs=pl.BlockSpec((tc, CONV_CH), lambda b, j: (b * nt + j, 0)),
        out_shape=jax.ShapeDtypeStruct((B * S, CONV_CH), BF16),
        scratch_shapes=[pltpu.VMEM((CONV_HALO + tc + SUBLANES, CONV_CH), F32),
                        pltpu.VMEM((SUBLANES, CONV_HALO + tc, CONV_CH), F32)],
        compiler_params=_cparams(("arbitrary", "arbitrary")),
        name="conv_prompt",
    )(cin, conv_w, conv_b.reshape(1, -1), ln_g.reshape(1, -1), ln_b.reshape(1, -1))


def _conv_sample_kernel(cpad_ref, w_ref, b_ref, g_ref, lb_ref, o_ref):
    bb, tp, _ = cpad_ref.shape
    t = tp - (CONV_W - 1)
    for b in range(bb):
        acc = jnp.zeros((t, CONV_CH), F32) + b_ref[...]
        for j in range(CONV_W):
            acc = acc + cpad_ref[b, j:j + t, :] * w_ref[j:j + 1, :]
        o_ref[b] = _ln_silu(acc, g_ref[...], lb_ref[...]).astype(BF16)


def _conv_sample(cpad, conv_w, conv_b, ln_g, ln_b):
    B, tp, _ = cpad.shape
    t = tp - (CONV_W - 1)
    bb = 8
    return pl.pallas_call(
        _conv_sample_kernel,
        grid=(B // bb,),
        in_specs=[pl.BlockSpec((bb, tp, CONV_CH), lambda i: (i, 0, 0)),
                  _const_spec((CONV_W, CONV_CH)), _const_spec((1, CONV_CH)),
                  _const_spec((1, CONV_CH)), _const_spec((1, CONV_CH))],
        out_specs=pl.BlockSpec((bb, t, CONV_CH), lambda i: (i, 0, 0)),
        out_shape=jax.ShapeDtypeStruct((B, t, CONV_CH), BF16),
        compiler_params=_cparams(("arbitrary",)),
        name="conv_sample",
    )(cpad, conv_w, conv_b.reshape(1, -1), ln_g.reshape(1, -1), ln_b.reshape(1, -1))


R_EXP = 0
R_GRP = N_EXPERTS


def _lane_min_index(mask, lane):
    return jnp.min(jnp.where(mask, lane, float(LANES)), axis=-1, keepdims=True)


def _router_gate(logits):
    lane = lax.broadcasted_iota(jnp.int32, logits.shape, 1).astype(F32)
    is_grp = (lane >= R_GRP) & (lane < R_GRP + N_GROUPS)
    gl = jnp.where(is_grp, logits, -jnp.inf)
    gmax = jnp.max(gl, axis=-1, keepdims=True)
    gexp = jnp.where(is_grp, jnp.exp(gl - gmax), 0.0)
    p_grp = gexp / jnp.sum(gexp, axis=-1, keepdims=True)
    g_top = jnp.max(p_grp, axis=-1, keepdims=True)
    g_idx = _lane_min_index(is_grp & (p_grp == g_top), lane) - R_GRP
    in_grp = (lane >= g_idx * EXP_PER_GROUP) & (lane < (g_idx + 1) * EXP_PER_GROUP)
    el = jnp.where(in_grp, logits, -jnp.inf)
    emax = jnp.max(el, axis=-1, keepdims=True)
    eexp = jnp.where(in_grp, jnp.exp(el - emax), 0.0)
    p_exp = eexp / jnp.sum(eexp, axis=-1, keepdims=True)
    p1 = jnp.max(jnp.where(in_grp, p_exp, -1.0), axis=-1, keepdims=True)
    i1 = _lane_min_index(in_grp & (p_exp == p1), lane)
    rest = in_grp & (lane != i1)
    p2 = jnp.max(jnp.where(rest, p_exp, -1.0), axis=-1, keepdims=True)
    i2 = _lane_min_index(rest & (p_exp == p2), lane)
    tot = p1 + p2
    return jnp.where(lane == i1, g_top * p1 / tot, 0.0) + jnp.where(lane == i2, g_top * p2 / tot, 0.0)


def _merge_kernel(x_ref, g1_ref, sc2_ref, sh2_ref, yc_ref, at_ref, gates_ref, wco_ref, wao_ref, wout_ref,
                  n2g_ref, wrh_ref, wrl_ref, br_ref, x1_ref, h2_ref, gate_ref):
    bb, tt, _ = x_ref.shape
    y_conv = jnp.dot(yc_ref[...], wco_ref[...], preferred_element_type=F32)
    y_attn = jnp.dot(at_ref[...], wao_ref[...], preferred_element_type=F32)
    m = gates_ref[:, :D_MODEL] * y_conv + gates_ref[:, D_MODEL:] * y_attn
    mo = jnp.dot(m.astype(BF16), wout_ref[...], preferred_element_type=F32)
    x1 = x_ref[...] + g1_ref[...] * mo.reshape(bb, tt, D_MODEL)
    x1_ref[...] = x1
    ms = jnp.mean(x1 * x1, axis=-1, keepdims=True)
    h2 = (x1 * lax.rsqrt(ms + EPS) * n2g_ref[...] * (1.0 + sc2_ref[...]) + sh2_ref[...]).reshape(bb * tt, D_MODEL)
    h2_hi = h2.astype(BF16)
    h2_ref[...] = h2_hi
    h2_lo = (h2 - h2_hi.astype(F32)).astype(BF16)
    logits = (jnp.dot(h2_hi, wrh_ref[...], preferred_element_type=F32)
              + jnp.dot(h2_lo, wrh_ref[...], preferred_element_type=F32)
              + jnp.dot(h2_hi, wrl_ref[...], preferred_element_type=F32)) + br_ref[...]
    gate_ref[...] = _router_gate(logits)


def _merge(x, g1, sc2, sh2, yc, attn, gates, w_conv_out, w_attn_o, w_out, norm2_g, w_grp, b_grp, w_exp, b_exp,
           bb, tt):
    B, T, _ = x.shape
    n = B * T
    tm = bb * tt
    nb_t = T // tt
    row = lambda i, j: (i * nb_t + j, 0)
    mod = lambda i, j: (i, 0, 0)
    pad = jnp.zeros((D_MODEL, LANES - N_EXPERTS - N_GROUPS), F32)
    wr = jnp.concatenate([w_exp, w_grp, pad], axis=1)
    wr_hi = wr.astype(BF16)
    br = jnp.concatenate([b_exp, b_grp, jnp.zeros((LANES - N_EXPERTS - N_GROUPS,), F32)]).reshape(1, LANES)
    return pl.pallas_call(
        _merge_kernel,
        grid=(B // bb, nb_t),
        in_specs=[pl.BlockSpec((bb, tt, D_MODEL), lambda i, j: (i, j, 0)),
                  pl.BlockSpec((bb, 1, D_MODEL), mod), pl.BlockSpec((bb, 1, D_MODEL), mod),
                  pl.BlockSpec((bb, 1, D_MODEL), mod),
                  pl.BlockSpec((tm, CONV_CH), row), pl.BlockSpec((tm, N_HEADS * HEAD_DIM), row),
                  pl.BlockSpec((tm, 2 * D_MODEL), row),
                  _const_spec((CONV_CH, D_MODEL)), _const_spec((N_HEADS * HEAD_DIM, D_MODEL)),
                  _const_spec((D_MODEL, D_MODEL)), _const_spec((1, 1, D_MODEL)),
                  _const_spec((D_MODEL, LANES)), _const_spec((D_MODEL, LANES)), _const_spec((1, LANES))],
        out_specs=[pl.BlockSpec((bb, tt, D_MODEL), lambda i, j: (i, j, 0)),
                   pl.BlockSpec((tm, D_MODEL), row), pl.BlockSpec((tm, LANES), row)],
        out_shape=[jax.ShapeDtypeStruct((B, T, D_MODEL), F32), jax.ShapeDtypeStruct((n, D_MODEL), BF16),
                   jax.ShapeDtypeStruct((n, LANES), F32)],
        compiler_params=_cparams(("arbitrary", "arbitrary")),
        name="merge_router",
    )(x, g1, sc2, sh2, yc, attn, gates, w_conv_out.astype(BF16), w_attn_o.astype(BF16), w_out.astype(BF16),
      norm2_g.reshape(1, 1, D_MODEL), wr_hi, (wr - wr_hi.astype(F32)).astype(BF16), br)


def _moe_kernel(h2_ref, gate_ref, w13_ref, w2_ref, x1_ref, g2_ref, o_ref, acc_ref):
    e = pl.program_id(2)
    bb, tt, _ = x1_ref.shape

    @pl.when(e == 0)
    def _():
        acc_ref[...] = jnp.zeros_like(acc_ref)

    ab = jnp.dot(h2_ref[...], w13_ref[0], preferred_element_type=F32)
    lane = lax.broadcasted_iota(jnp.int32, gate_ref.shape, 1)
    gcol = jnp.sum(jnp.where(lane == e, gate_ref[...], 0.0), axis=-1, keepdims=True)
    hid = _silu(ab[:, :EXPERT_FF]) * ab[:, EXPERT_FF:] * gcol
    acc_ref[...] += jnp.dot(hid.astype(BF16), w2_ref[0], preferred_element_type=F32)

    @pl.when(e == N_EXPERTS - 1)
    def _():
        o_ref[...] = x1_ref[...] + g2_ref[...] * acc_ref[...].reshape(bb, tt, D_MODEL)


def _moe(h2, gate, x1, g2, w13, w2b, bb, tt):
    B, T, _ = x1.shape
    tm = bb * tt
    nb_t = T // tt
    row = lambda i, j, e: (i * nb_t + j, 0)
    return pl.pallas_call(
        _moe_kernel,
        grid=(B // bb, nb_t, N_EXPERTS),
        in_specs=[pl.BlockSpec((tm, D_MODEL), row), pl.BlockSpec((tm, LANES), row),
                  pl.BlockSpec((1, D_MODEL, 2 * EXPERT_FF), lambda i, j, e: (e, 0, 0)),
                  pl.BlockSpec((1, EXPERT_FF, D_MODEL), lambda i, j, e: (e, 0, 0)),
                  pl.BlockSpec((bb, tt, D_MODEL), lambda i, j, e: (i, j, 0)),
                  pl.BlockSpec((bb, 1, D_MODEL), lambda i, j, e: (i, 0, 0))],
        out_specs=pl.BlockSpec((bb, tt, D_MODEL), lambda i, j, e: (i, j, 0)),
        out_shape=jax.ShapeDtypeStruct((B, T, D_MODEL), F32),
        scratch_shapes=[pltpu.VMEM((tm, D_MODEL), F32)],
        compiler_params=_cparams(("arbitrary", "arbitrary", "arbitrary")),
        name="moe_dense",
    )(h2, gate, w13, w2b, x1, g2)


def _lane_bcast(col):
    return jnp.broadcast_to(col, (col.shape[0], LANES))


def _row_any(flag_rows):
    return jnp.max(jnp.where(flag_rows, 1.0, 0.0)) > 0.0


def _select_rows(sc_ref, nch, cw, n_keep, jl_ref):
    R = sc_ref.shape[0]
    ng = cw // LANES
    kf = float(n_keep)
    inf = jnp.full((R, LANES), jnp.inf, F32)
    zero = jnp.zeros((R, LANES), F32)
    lane = lax.broadcasted_iota(jnp.int32, (R, LANES), 1).astype(F32)

    def scan(init, fn):
        def body(c, carry):
            base = pl.multiple_of(c * cw, LANES)
            for j in range(ng):
                blk = sc_ref[:, pl.ds(base + j * LANES, LANES)]
                carry = fn(carry, blk, (base + j * LANES).astype(F32))
            return carry
        return lax.fori_loop(0, nch, body, init)

    rsum = lambda a: _lane_bcast(jnp.sum(a, axis=-1, keepdims=True))
    rmin = lambda a: _lane_bcast(jnp.min(a, axis=-1, keepdims=True))
    rmax = lambda a: _lane_bcast(jnp.max(a, axis=-1, keepdims=True))

    mx, mn = scan((-inf, inf), lambda c, x, _: (jnp.maximum(c[0], x),
                                                 jnp.minimum(c[1], jnp.where(x == -jnp.inf, jnp.inf, x))))
    hi0, lo0 = rmax(mx), rmin(mn)

    def bisect(_, lh):
        lo, hi = lh
        mid = 0.5 * lo + 0.5 * hi
        cnt = rsum(scan(zero, lambda c, x, _: c + jnp.where(x >= mid, 1.0, 0.0)))
        ge = cnt >= kf
        return jnp.where(ge, mid, lo), jnp.where(ge, hi, mid)

    lo, _ = lax.fori_loop(0, N_BISECT, bisect, (lo0, hi0))
    tau0 = rmin(scan(inf, lambda c, x, _: jnp.minimum(c, jnp.where(x >= lo, x, jnp.inf))))

    def refine_pass(tau):
        def fn(c, x, _):
            gt = x > tau
            return (c[0] + jnp.where(gt, 1.0, 0.0), jnp.minimum(c[1], jnp.where(gt, x, jnp.inf)),
                    c[2] + jnp.where(x == tau, 1.0, 0.0))
        cgt, nxt, ceq = scan((zero, inf, zero), fn)
        return rsum(cgt), rmin(nxt), rsum(ceq)

    def refine_body(state):
        tau, _, _, _ = state
        cgt, nxt, ceq = refine_pass(tau)
        up = cgt >= kf
        return jnp.where(up, nxt, tau), cgt, ceq, _row_any(up)

    tau, cgt, ceq, _ = lax.while_loop(lambda s: s[3], refine_body, (tau0, zero, zero, jnp.bool_(True)))

    need = kf - cgt
    excess = ceq > need
    jl_ref[...] = inf

    @pl.when(_row_any(excess))
    def _():
        def idx_bisect(_, lh):
            lo_j, hi_j = lh
            mid = jnp.floor((lo_j + hi_j) * 0.5)
            cnt = rsum(scan(zero, lambda c, x, col0: c + jnp.where((x == tau) & (col0 + lane <= mid), 1.0, 0.0)))
            ge = cnt >= need
            return jnp.where(ge, lo_j, mid), jnp.where(ge, mid, hi_j)

        ncol = (nch * cw).astype(F32) if hasattr(nch, "astype") else float(nch * cw)
        _, hi_j = lax.fori_loop(0, 14, idx_bisect, (zero - 1.0, zero + (ncol - 1.0)))
        jl_ref[...] = jnp.where(excess, hi_j, jnp.inf)

    return tau


ATT_TQ = 128
ATT_TK = 1024
IDX_TN = 256
SEL_CHUNK = 1024
BIAS_ROWS = 16

_NT = (((1,), (1,)), ((), ()))


def _kv_lane_offset(head):
    return ((head // 2) % 2) * HEAD_DIM


def _assemble_heads(per_head):
    lane = lax.broadcasted_iota(jnp.int32, per_head[0].shape, 1)
    cols = []
    for c in range(N_HEADS // 2):
        a, b = per_head[2 * c], per_head[2 * c + 1]
        if _kv_lane_offset(2 * c) == 0:
            b = pltpu.roll(b, HEAD_DIM, 1)
        else:
            a = pltpu.roll(a, HEAD_DIM, 1)
        cols.append(jnp.where(lane < HEAD_DIM, a, b))
    return jnp.concatenate(cols, axis=1)


def _attn_prompt_kernel(q_ref, qi_ref, kiw_ref, kib_ref, kb_ref, vb_ref, o_ref,
                        sc_ref, wb_ref, jl_ref, tau_ref, bias_ref, m_ref, al_ref, pm_ref, acc_ref, s_ref, p_ref,
                        *, n_keep):
    tq, tk = ATT_TQ, ATT_TK
    qb = pl.program_id(1)
    t0 = qb * tq
    nkb = (t0 + tq + tk - 1) // tk

    for h in range(IDX_HEADS):
        wb_ref[h] = _lane_bcast(kiw_ref[:, IDX_DIM + h:IDX_DIM + h + 1])

    row_g = t0 + lax.broadcasted_iota(jnp.int32, (tq, IDX_TN), 0)
    col_l = lax.broadcasted_iota(jnp.int32, (tq, IDX_TN), 1)

    def score_body(c, _):
        for j in range(tk // IDX_TN):
            c0 = pl.multiple_of(c * tk + j * IDX_TN, IDX_TN)
            kt = kib_ref[0, pl.ds(c0, IDX_TN), :]
            acc = jnp.zeros((tq, IDX_TN), F32)
            for h in range(IDX_HEADS):
                y = lax.dot_general(qi_ref[:, h * LANES:(h + 1) * LANES], kt, _NT, preferred_element_type=F32)
                wb = wb_ref[h]
                acc = acc + jnp.maximum(y, 0.0) * jnp.concatenate([wb] * (IDX_TN // LANES), axis=1)
            sc_ref[:, pl.ds(c0, IDX_TN)] = jnp.where(c0 + col_l <= row_g, acc * IDX_SCALE, -jnp.inf)
        return 0

    lax.fori_loop(0, nkb, score_body, 0)

    per_chunk = SEL_CHUNK // tk
    for k in range(1, per_chunk):
        @pl.when(nkb % per_chunk == k)
        def _():
            for j in range(k, per_chunk):
                sc_ref[:, pl.ds(pl.multiple_of((nkb - k + j) * tk, tk), tk)] = jnp.full((tq, tk), -jnp.inf, F32)

    tau_ref[...] = _select_rows(sc_ref, (nkb + per_chunk - 1) // per_chunk, SEL_CHUNK, n_keep, jl_ref)

    m_ref[...] = jnp.full(m_ref.shape, NEG_BIG, F32)
    acc_ref[...] = jnp.zeros(acc_ref.shape, F32)
    col_k = lax.broadcasted_iota(jnp.int32, (BIAS_ROWS, tk), 1).astype(F32)
    heads_per_pair = N_HEADS // (N_KV_HEADS // 2)
    ncg = tk // LANES

    def tile_lanes(a):
        return jnp.concatenate([a] * ncg, axis=1)

    def attn_body(c, _):
        c0 = pl.multiple_of(c * tk, tk)
        col = col_k + c0.astype(F32)
        for r0 in range(0, tq, BIAS_ROWS):
            rows = slice(r0, r0 + BIAS_ROWS)
            x = sc_ref[rows, pl.ds(c0, tk)]
            tau = tile_lanes(tau_ref[rows, :])
            sel = (x > tau) | ((x == tau) & (col <= tile_lanes(jl_ref[rows, :])))
            bias_ref[rows, :] = jnp.where(sel, 0.0, NEG_BIG)

        def logits(h):
            pair = h // heads_per_pair
            kt = kb_ref[0, pl.ds(c0, tk), pair * LANES:(pair + 1) * LANES]
            s = lax.dot_general(q_ref[:, h * LANES:(h + 1) * LANES], kt, _NT, preferred_element_type=F32)
            s_ref[h] = s + bias_ref[...]
            pm = s_ref[h, :, 0:LANES]
            for j in range(1, ncg):
                pm = jnp.maximum(pm, s_ref[h, :, j * LANES:(j + 1) * LANES])
            pm_ref[h] = pm

        def new_max(h):
            m_old = m_ref[h]
            m_new = jnp.maximum(m_old, _lane_bcast(jnp.max(pm_ref[h], axis=-1, keepdims=True)))
            al_ref[h] = jnp.exp2(m_old - m_new)
            m_ref[h] = m_new

        def probs(h):
            p_ref[h] = jnp.exp2(s_ref[h] - tile_lanes(m_ref[h])).astype(BF16)

        def accumulate(h):
            pair = h // heads_per_pair
            vt = vb_ref[0, pl.ds(c0, tk), pair * 2 * LANES:(pair + 1) * 2 * LANES]
            alpha = al_ref[h]
            acc_ref[h] = (jnp.concatenate([alpha, alpha], axis=1) * acc_ref[h]
                          + jnp.dot(p_ref[h], vt, preferred_element_type=F32))

        half = N_HEADS // 2
        for h in range(half):
            logits(h)
        for h in range(half):
            new_max(h)
        for h in range(half):
            logits(half + h)
            probs(h)
        for h in range(half):
            new_max(half + h)
        for h in range(half):
            accumulate(h)
            probs(half + h)
        for h in range(half):
            accumulate(half + h)
        return 0

    lax.fori_loop(0, nkb, attn_body, 0)
    o_ref[...] = _assemble_heads([acc_ref[h, :, :LANES] / acc_ref[h, :, LANES:]
                                  for h in range(N_HEADS)]).astype(BF16)


def _attn_prompt(q, qi, kiw, kib, kb, vb, B, S):
    tq = ATT_TQ
    assert S % SEL_CHUNK == 0 and SEL_CHUNK % ATT_TK == 0 and ATT_TK % tq == 0
    nq = S // tq
    n_keep = min(TOPK_MAX, S // 4)
    row = lambda b, j: (b * nq + j, 0)
    per_b = lambda b, j: (b, 0, 0)
    return pl.pallas_call(
        functools.partial(_attn_prompt_kernel, n_keep=n_keep),
        grid=(B, nq),
        in_specs=[pl.BlockSpec((tq, N_HEADS * LANES), row), pl.BlockSpec((tq, IDX_HEADS * LANES), row),
                  pl.BlockSpec((tq, LANES), row),
                  pl.BlockSpec((1, S, LANES), per_b), pl.BlockSpec((1, S, KV_W), per_b),
                  pl.BlockSpec((1, S, 2 * KV_W), per_b)],
        out_specs=pl.BlockSpec((tq, N_HEADS * HEAD_DIM), row),
        out_shape=jax.ShapeDtypeStruct((B * S, N_HEADS * HEAD_DIM), BF16),
        scratch_shapes=[pltpu.VMEM((tq, S), F32), pltpu.VMEM((IDX_HEADS, tq, LANES), F32),
                        pltpu.VMEM((tq, LANES), F32), pltpu.VMEM((tq, LANES), F32)]
                       + [pltpu.VMEM((tq, ATT_TK), F32)]
                       + [pltpu.VMEM((N_HEADS, tq, LANES), F32)] * 3
                       + [pltpu.VMEM((N_HEADS, tq, 2 * LANES), F32)]
                       + [pltpu.VMEM((N_HEADS, tq, ATT_TK), F32), pltpu.VMEM((N_HEADS, tq, ATT_TK), BF16)],
        compiler_params=_cparams(("arbitrary", "arbitrary")),
        name="attn_prompt",
    )(q, qi, kiw, kib.reshape(B, S, LANES), kb.reshape(B, S, KV_W), vb.reshape(B, S, 2 * KV_W))


SCORE_PAGES = 64
ATTN_PAGES = 64
SEL_ROWS = 128
SEL_CW = 640


def _sample_scores_kernel(pt_ref, qi_ref, w_ref, kin_ref, *rest):
    del pt_ref
    page_refs, o_ref = rest[:SCORE_PAGES], rest[SCORE_PAGES]
    g = pl.program_id(1)
    t = o_ref.shape[1]
    qi = qi_ref[0]
    wb = _lane_bcast(w_ref[0])

    def page_scores(kpage_t):
        y = jnp.dot(qi, kpage_t.astype(BF16), preferred_element_type=F32)
        z = jnp.maximum(y, 0.0) * wb
        tot = z[0:t]
        for h in range(1, IDX_HEADS):
            tot = tot + z[h * t:(h + 1) * t]
        return tot * IDX_SCALE

    for j in range(SCORE_PAGES):
        col0 = pl.multiple_of((g * SCORE_PAGES + j) * PAGE_SIZE, PAGE_SIZE)
        o_ref[0, :, pl.ds(col0, PAGE_SIZE)] = page_scores(page_refs[j][0])

    @pl.when(g == 0)
    def _():
        past = o_ref.shape[2] - PAGE_SIZE
        qpos = lax.broadcasted_iota(jnp.int32, (t, PAGE_SIZE), 0)
        kpos = lax.broadcasted_iota(jnp.int32, (t, PAGE_SIZE), 1)
        o_ref[0, :, past:] = jnp.where(kpos <= qpos, page_scores(kin_ref[0]), -jnp.inf)


def _sample_scores(page_table, qi_s, w_s, ki_new_t, cache_kidx_t):
    B, n_pages = page_table.shape
    t = qi_s.shape[1] // IDX_HEADS
    past = n_pages * PAGE_SIZE
    per_b = lambda b, g, pt: (b, 0, 0)
    page_specs = [pl.BlockSpec((1, IDX_DIM, PAGE_SIZE),
                               functools.partial(lambda b, g, pt, j: (pt[b, g * SCORE_PAGES + j], 0, 0), j=j))
                  for j in range(SCORE_PAGES)]
    return pl.pallas_call(
        _sample_scores_kernel,
        grid_spec=pltpu.PrefetchScalarGridSpec(
            num_scalar_prefetch=1,
            grid=(B, n_pages // SCORE_PAGES),
            in_specs=[pl.BlockSpec((1, IDX_HEADS * t, IDX_DIM), per_b), pl.BlockSpec((1, IDX_HEADS * t, 1), per_b),
                      pl.BlockSpec((1, IDX_DIM, PAGE_SIZE), per_b)] + page_specs,
            out_specs=pl.BlockSpec((1, t, past + PAGE_SIZE), per_b)),
        out_shape=jax.ShapeDtypeStruct((B, t, past + PAGE_SIZE), F32),
        compiler_params=_cparams(("arbitrary", "arbitrary")),
        name="sample_scores",
    )(page_table, qi_s, w_s, ki_new_t, *([cache_kidx_t] * SCORE_PAGES))


def _select_kernel(sc_ref, tau_ref, jl_ref, *, n_keep):
    tau_ref[...] = _select_rows(sc_ref, sc_ref.shape[1] // SEL_CW, SEL_CW, n_keep, jl_ref)


def _sample_select(scores2d, n_keep):
    r, l = scores2d.shape
    rows = min(SEL_ROWS, r)
    row = lambda i: (i, 0)
    return pl.pallas_call(
        functools.partial(_select_kernel, n_keep=n_keep),
        grid=(r // rows,),
        in_specs=[pl.BlockSpec((rows, l), row)],
        out_specs=[pl.BlockSpec((rows, LANES), row), pl.BlockSpec((rows, LANES), row)],
        out_shape=[jax.ShapeDtypeStruct((r, LANES), F32), jax.ShapeDtypeStruct((r, LANES), F32)],
        compiler_params=_cparams(("arbitrary",)),
        name="sample_select",
    )(scores2d)


def _sample_attn_kernel(pt_ref, q_ref, sc_ref, tau_ref, jl_ref, kn_ref, vn_ref, *rest):
    del pt_ref
    k_refs, v_refs = rest[:ATTN_PAGES], rest[ATTN_PAGES:2 * ATTN_PAGES]
    o_ref, m_ref, l_ref, pm_ref, acc_ref, kbuf_ref, vbuf_ref, s_ref, p_ref = rest[2 * ATTN_PAGES:]
    g = pl.program_id(1)
    t = sc_ref.shape[1]
    group = N_HEADS // N_KV_HEADS
    span = ATTN_PAGES * PAGE_SIZE

    @pl.when(g == 0)
    def _():
        m_ref[...] = jnp.full(m_ref.shape, NEG_BIG, F32)
        l_ref[...] = jnp.zeros(l_ref.shape, F32)
        acc_ref[...] = jnp.zeros(acc_ref.shape, F32)

    tau = tau_ref[0][:, 0:1]
    jl = jl_ref[0][:, 0:1]

    def attend(k_of, v_of, x, col0f):
        col = lax.broadcasted_iota(jnp.int32, x.shape, 1).astype(F32) + col0f
        sel = (x > tau) | ((x == tau) & (col <= jl))
        bias = jnp.concatenate([jnp.where(sel, 0.0, NEG_BIG)] * group, axis=0)
        n = x.shape[1]
        ncg = n // LANES
        kv_rows = [slice(kv * group * t, (kv + 1) * group * t) for kv in range(N_KV_HEADS)]

        for kv, rows in enumerate(kv_rows):
            s = jnp.dot(q_ref[0, rows, :], k_of(kv), preferred_element_type=F32) + bias
            s_ref[rows, 0:n] = s
            pm = s[:, 0:LANES]
            for j in range(1, ncg):
                pm = jnp.maximum(pm, s[:, j * LANES:(j + 1) * LANES])
            pm_ref[rows] = pm

        m_old = m_ref[...]
        m_new = jnp.maximum(m_old, _lane_bcast(jnp.max(pm_ref[...], axis=-1, keepdims=True)))
        alpha = jnp.exp2(m_old - m_new)
        m_ref[...] = m_new

        for rows in kv_rows:
            p = jnp.exp2(s_ref[rows, 0:n] - jnp.concatenate([m_new[rows]] * ncg, axis=1))
            p_ref[rows, 0:n] = p.astype(BF16)
            ps = p[:, 0:LANES]
            for j in range(1, ncg):
                ps = ps + p[:, j * LANES:(j + 1) * LANES]
            pm_ref[rows] = ps

        l_ref[...] = alpha * l_ref[...] + _lane_bcast(jnp.sum(pm_ref[...], axis=-1, keepdims=True))

        for kv, rows in enumerate(kv_rows):
            pv = lax.dot_general(p_ref[rows, 0:n], v_of(kv), _NT, preferred_element_type=F32)
            acc_ref[rows] = alpha[rows, :HEAD_DIM] * acc_ref[rows] + pv

    for j in range(ATTN_PAGES):
        kbuf_ref[:, :, j * PAGE_SIZE:(j + 1) * PAGE_SIZE] = k_refs[j][0].astype(BF16)
        vbuf_ref[:, :, j * PAGE_SIZE:(j + 1) * PAGE_SIZE] = v_refs[j][0].astype(BF16)
    col0 = pl.multiple_of(g * span, span)
    attend(lambda kv: kbuf_ref[kv], lambda kv: vbuf_ref[kv], sc_ref[0, :, pl.ds(col0, span)], col0.astype(F32))

    @pl.when(g == pl.num_programs(1) - 1)
    def _():
        past = sc_ref.shape[2] - PAGE_SIZE
        attend(lambda kv: kn_ref[0, kv], lambda kv: vn_ref[0, kv], sc_ref[0, :, past:], float(past))
        o_ref[0] = acc_ref[...] / l_ref[:, :HEAD_DIM]


def _sample_attn(page_table, q_s, scores, tau, jl, k_new_t, v_new_t, cache_k_t, cache_v_t):
    B, n_pages = page_table.shape
    t = scores.shape[1]
    rows = N_HEADS * t
    per_b3 = lambda b, g, pt: (b, 0, 0)
    per_b4 = lambda b, g, pt: (b, 0, 0, 0)
    page_blk = (1, N_KV_HEADS, HEAD_DIM, PAGE_SIZE)

    def page_specs():
        return [pl.BlockSpec(page_blk,
                             functools.partial(lambda b, g, pt, j: (pt[b, g * ATTN_PAGES + j], 0, 0, 0), j=j))
                for j in range(ATTN_PAGES)]

    return pl.pallas_call(
        _sample_attn_kernel,
        grid_spec=pltpu.PrefetchScalarGridSpec(
            num_scalar_prefetch=1,
            grid=(B, n_pages // ATTN_PAGES),
            in_specs=[pl.BlockSpec((1, rows, HEAD_DIM), per_b3), pl.BlockSpec((1, t, scores.shape[2]), per_b3),
                      pl.BlockSpec((1, t, LANES), per_b3), pl.BlockSpec((1, t, LANES), per_b3),
                      pl.BlockSpec(page_blk, per_b4), pl.BlockSpec(page_blk, per_b4)]
                     + page_specs() + page_specs(),
            out_specs=pl.BlockSpec((1, rows, HEAD_DIM), per_b3),
            scratch_shapes=[pltpu.VMEM((rows, LANES), F32)] * 3 + [pltpu.VMEM((rows, HEAD_DIM), F32)]
                           + [pltpu.VMEM((N_KV_HEADS, HEAD_DIM, ATTN_PAGES * PAGE_SIZE), BF16)] * 2
                           + [pltpu.VMEM((rows, ATTN_PAGES * PAGE_SIZE), F32),
                              pltpu.VMEM((rows, ATTN_PAGES * PAGE_SIZE), BF16)]),
        out_shape=jax.ShapeDtypeStruct((B, rows, HEAD_DIM), F32),
        compiler_params=_cparams(("arbitrary", "arbitrary")),
        name="sample_attn",
    )(page_table, q_s, scores, tau, jl, k_new_t, v_new_t,
      *([cache_k_t] * ATTN_PAGES), *([cache_v_t] * ATTN_PAGES))


def _sample_attention(sr, B, T, page_table, cache_k_l, cache_v_l, cache_kidx_l):
    n_pages = page_table.shape[1]
    n_keep = min(TOPK_MAX, (n_pages * PAGE_SIZE + T) // 4)

    def head_major(a, lane_offset):
        a = a.reshape(B, T, N_HEADS, LANES)
        heads = [a[:, :, h, lane_offset(h):lane_offset(h) + HEAD_DIM] for h in range(N_HEADS)]
        return jnp.stack(heads, axis=1).reshape(B, N_HEADS * T, HEAD_DIM)

    def new_transposed(a, heads):
        a = jnp.transpose(a.reshape(B, T, heads, -1), (0, 2, 3, 1))
        return jnp.pad(a, ((0, 0), (0, 0), (0, 0), (0, PAGE_SIZE - T)))

    kidx_t = jnp.transpose(cache_kidx_l, (0, 2, 1))
    k_t = jnp.transpose(cache_k_l, (0, 2, 3, 1))
    v_t = jnp.transpose(cache_v_l, (0, 2, 3, 1))

    qi_s = head_major(sr["qi"], lambda h: 0)
    w_s = jnp.transpose(sr["kiw"][:, IDX_DIM:IDX_DIM + IDX_HEADS].reshape(B, T, IDX_HEADS), (0, 2, 1))
    w_s = w_s.reshape(B, IDX_HEADS * T, 1)
    scores = _sample_scores(page_table, qi_s, w_s, new_transposed(sr["kiw"][:, :IDX_DIM], 1)[:, 0], kidx_t)
    tau, jl = _sample_select(scores.reshape(B * T, -1), n_keep)
    o = _sample_attn(page_table, head_major(sr["q"], _kv_lane_offset), scores, tau.reshape(B, T, LANES),
                     jl.reshape(B, T, LANES), new_transposed(sr["kb"], N_KV_HEADS),
                     new_transposed(sr["v"].astype(BF16), N_KV_HEADS), k_t, v_t)
    o = jnp.transpose(o.reshape(B, N_HEADS, T, HEAD_DIM), (0, 2, 1, 3))
    return o.reshape(B * T, N_HEADS * HEAD_DIM).astype(BF16)


def _layer(x, mods, conv_past, attn_fn, p, tiles):
    B, T, _ = x.shape
    sh1, sc1, g1, sh2, sc2, g2 = mods
    pr = _inproj(x, sc1, sh1, p["norm1_g"], p["w_pack"], p["q_norm_g"], p["k_norm_g"], *tiles["inproj"])
    cin = pr["cin"].reshape(B, T, CONV_CH)
    if conv_past is None:
        yc = _conv_prompt(pr["cin"], B, T, p["conv_w"], p["conv_b"], p["ln_g"], p["ln_b"])
        conv_state = cin[:, T - (CONV_W - 1):]
    else:
        cpad = jnp.concatenate([conv_past, cin], axis=1)
        yc = _conv_sample(cpad, p["conv_w"], p["conv_b"], p["ln_g"], p["ln_b"]).reshape(B * T, CONV_CH)
        conv_state = cpad[:, T:]
    attn = attn_fn(pr)
    x1, h2, gate = _merge(x, g1, sc2, sh2, yc, attn, pr["gates"], p["w_conv_out"], p["w_attn_o"], p["w_out"],
                          p["norm2_g"], p["w_grp"], p["b_grp"], p["w_exp"], p["b_exp"], *tiles["merge"])
    y = _moe(h2, gate, x1, g2, p["w13"], p["w2b"], *tiles["moe"])
    k = pr["k"].reshape(B, T, N_KV_HEADS, HEAD_DIM)
    v = pr["v"].reshape(B, T, N_KV_HEADS, HEAD_DIM)
    ki = pr["kiw"][:, :IDX_DIM].reshape(B, T, IDX_DIM)
    return y, k, v, ki, conv_state


def kernel(x_prompt, x_sample, cache_k, cache_v, cache_kidx, state_conv, page_table, c_prompt, c_sample,
           w_ada, b_ada, norm1_g, w_in, q_norm_g, k_norm_g, conv_w, conv_b, ln_g, ln_b, w_conv_out, w_attn_o,
           w_out, norm2_g, w_grp, b_grp, w_exp, b_exp, w1, w3, w2):
    depth = w_ada.shape[0]
    Bp, S, _ = x_prompt.shape
    Bs, T, _ = x_sample.shape
    xp, xs = x_prompt, x_sample
    outs = [[] for _ in range(8)]
    c_all = jnp.concatenate([c_prompt, c_sample], axis=0)
    c_all = jnp.pad(c_all, ((0, (-c_all.shape[0]) % 16), (0, 0)))
    tiles_p = {"inproj": (1, min(256, S)), "merge": (1, min(512, S)), "moe": (1, min(1024, S))}
    tiles_s = {"inproj": (min(32, Bs), T), "merge": (min(64, Bs), T), "moe": (min(128, Bs), T)}
    for l in range(depth):
        p = dict(norm1_g=norm1_g[l], w_pack=_pack_w_in(w_in[l]), q_norm_g=q_norm_g[l], k_norm_g=k_norm_g[l],
                 conv_w=conv_w[l], conv_b=conv_b[l], ln_g=ln_g[l], ln_b=ln_b[l], w_conv_out=w_conv_out[l],
                 w_attn_o=w_attn_o[l], w_out=w_out[l], norm2_g=norm2_g[l], w_grp=w_grp[l], b_grp=b_grp[l],
                 w_exp=w_exp[l], b_exp=b_exp[l],
                 w13=jnp.concatenate([w1[l], w3[l]], axis=-1).astype(BF16), w2b=w2[l].astype(BF16))
        mod = _ada(c_all, w_ada[l], b_ada[l])
        mods = [m[:, None, :] for m in jnp.split(mod, 6, axis=-1)]
        mods_p = [m[:Bp] for m in mods]
        mods_s = [m[Bp:Bp + Bs] for m in mods]

        attn_p = lambda pr: _attn_prompt(pr["q"], pr["qi"], pr["kiw"], pr["kib"], pr["kb"], pr["vb"], Bp, S)
        xp, k, v, ki, cv = _layer(xp, mods_p, None, attn_p, p, tiles_p)
        for o, a in zip(outs[:4], (k, v, ki, cv)):
            o.append(a)

        attn_s = lambda pr: _sample_attention(pr, Bs, T, page_table, cache_k[l], cache_v[l], cache_kidx[l])
        xs, k, v, ki, cv = _layer(xs, mods_s, state_conv[l], attn_s, p, tiles_s)
        for o, a in zip(outs[4:], (k, v, ki, cv)):
            o.append(a)
    return (xp, xs) + tuple(jnp.stack(o) for o in outs)
```
